```python
import jax, jax.numpy as jnp
from jax import lax
import numpy as np

D_MODEL = 2048
BATCH = 32
SEQ = 256
DEPTH = 4
DEC_BATCH = 4
DEC_SEQ = 2048
PAST_LEN = 512

GRID_W = 64
N_EVEN = (DEPTH + 1) // 2
N_ODD = DEPTH // 2
CONV_CH = D_MODEL // 2
CONV_W = 31
HEAD_DIM = 128
N_Q_HEADS = (D_MODEL // 2) // HEAD_DIM
N_KV_HEADS = N_Q_HEADS // 4
Q_BLOCK = 128
ROPE_THETA = 10000.0
EVEN_IN = 2 * CONV_CH + (N_Q_HEADS + 2 * N_KV_HEADS) * HEAD_DIM
EVEN_OUT = CONV_CH + N_Q_HEADS * HEAD_DIM
GLA_HEADS = 4
GLA_DK = (D_MODEL // 2) // GLA_HEADS
GLA_DV = D_MODEL // GLA_HEADS
GLA_RANK = 16
GLA_TAU = 16.0
GLA_CHUNK = 64
GLA_HK = GLA_HEADS * GLA_DK
GLA_HV = GLA_HEADS * GLA_DV
ODD_IN = 2 * GLA_HK + 2 * GLA_HV + 2 * GLA_RANK
D_FF = 11 * D_MODEL // 4
FFN_CONV_W = 3
EPS = 1e-6

kernel_name = "hybrid_diffusion_prefix_step"


def _rms(x, g):
    xf = x.astype(jnp.float32)
    y = xf * lax.rsqrt(jnp.mean(xf * xf, -1, keepdims=True) + EPS)
    return (y * g.astype(jnp.float32)).astype(x.dtype)


def _layernorm(x, g, b):
    xf = x.astype(jnp.float32)
    mu = jnp.mean(xf, -1, keepdims=True)
    xc = xf - mu
    y = xc * lax.rsqrt(jnp.mean(xc * xc, -1, keepdims=True) + EPS)
    return (y * g.astype(jnp.float32) + b.astype(jnp.float32)).astype(x.dtype)


def _dwconv(x, w, b):
    K, C = w.shape
    y = lax.conv_general_dilated(x, w[:, None, :].astype(x.dtype), (1,), [(K // 2, K // 2)],
                                 dimension_numbers=('NWC', 'WIO', 'NWC'), feature_group_count=C)
    return y + b


def _axial_angles(T):
    rows = T // GRID_W
    row = jnp.repeat(jnp.arange(rows), GRID_W).astype(jnp.float32)
    col = jnp.tile(jnp.arange(GRID_W), rows).astype(jnp.float32)
    n = HEAD_DIM // 4
    inv = ROPE_THETA ** (-jnp.arange(n, dtype=jnp.float32) / n)
    ang = jnp.concatenate([row[:, None] * inv, col[:, None] * inv], -1)
    return jnp.cos(ang), jnp.sin(ang)


def _rope(x, cos, sin):
    xf = x.astype(jnp.float32).reshape(*x.shape[:-1], HEAD_DIM // 2, 2)
    x0, x1 = xf[..., 0], xf[..., 1]
    c = cos[None, :, None, :]
    s = sin[None, :, None, :]
    out = jnp.stack([x0 * c - x1 * s, x0 * s + x1 * c], -1)
    return out.reshape(x.shape).astype(x.dtype)


def _attend(q, k, v):
    B, Tq, Hq, hd = q.shape
    G = Hq // N_KV_HEADS
    nb = Tq // Q_BLOCK
    qb = q.reshape(B, nb, Q_BLOCK, N_KV_HEADS, G, hd).transpose(1, 0, 2, 3, 4, 5)
    kf = k.astype(jnp.float32)
    vf = v.astype(jnp.float32)
    scale = hd ** -0.5

    def blk(qi):
        s = jnp.einsum('bqkgd,bskd->bkgqs', qi.astype(jnp.float32), kf) * scale
        p = jax.nn.softmax(s, axis=-1)
        return jnp.einsum('bkgqs,bskd->bqkgd', p, vf).astype(q.dtype)

    o = lax.map(blk, qb)
    return o.transpose(1, 0, 2, 3, 4, 5).reshape(B, Tq, Hq * hd)


def _even_proj(h, w_in, cw, cb, ln_g, ln_b, qg, kg):
    B, T, _ = h.shape
    p = h @ w_in
    i0 = 2 * CONV_CH
    i1 = i0 + N_Q_HEADS * HEAD_DIM
    i2 = i1 + N_KV_HEADS * HEAD_DIM
    a = p[..., :CONV_CH] * jax.nn.sigmoid(p[..., CONV_CH:i0])
    a = jax.nn.silu(_layernorm(_dwconv(a, cw, cb), ln_g, ln_b))
    q = _rms(p[..., i0:i1].reshape(B, T, N_Q_HEADS, HEAD_DIM), qg)
    k = _rms(p[..., i1:i2].reshape(B, T, N_KV_HEADS, HEAD_DIM), kg)
    v = p[..., i2:].reshape(B, T, N_KV_HEADS, HEAD_DIM)
    return a, q, k, v


def _gla_proj(h, w_in, w_gate, b_gate):
    B, T, _ = h.shape
    p = h @ w_in
    s0 = GLA_HK
    s1 = 2 * GLA_HK
    s2 = s1 + GLA_HV
    s3 = s2 + GLA_HV
    shp = (B, T, GLA_HEADS, -1)
    q = p[..., :s0].reshape(shp) * (GLA_DK ** -0.5)
    k = p[..., s0:s1].reshape(shp)
    v = p[..., s1:s2].reshape(shp)
    r = p[..., s2:s3].reshape(shp)
    z = p[..., s3:]
    z_f = z[..., :GLA_RANK] @ w_gate[0] + b_gate[0]
    z_b = z[..., GLA_RANK:] @ w_gate[1] + b_gate[1]
    lg_f = (jax.nn.log_sigmoid(z_f.astype(jnp.float32)) / GLA_TAU).reshape(shp)
    lg_b = (jax.nn.log_sigmoid(z_b.astype(jnp.float32)) / GLA_TAU).reshape(shp)
    return q, k, v, r, lg_f, lg_b


def _gla_scan(q, k, v, lg, h0):
    B, T, H, _ = q.shape
    DV = v.shape[-1]
    L = GLA_CHUNK
    n = T // L

    def split(a):
        return a.reshape(B, n, L, H, a.shape[-1]).transpose(1, 0, 3, 2, 4).astype(jnp.float32)

    mask = jnp.tril(jnp.ones((L, L), dtype=bool))

    def step(h, inp):
        qc, kc, vc, gc = inp
        b = jnp.cumsum(gc, axis=2)
        qe = qc * jnp.exp(b)
        ke = kc * jnp.exp(-b)
        a = jnp.where(mask, jnp.einsum('bhld,bhsd->bhls', qe, ke), 0.0)
        o = jnp.einsum('bhld,bhdv->bhlv', qe, h) + jnp.einsum('bhls,bhsv->bhlv', a, vc)
        bl = b[:, :, -1:, :]
        h = jnp.exp(bl[:, :, 0, :])[..., None] * h + jnp.einsum('bhld,bhlv->bhdv', kc * jnp.exp(bl - b), vc)
        return h, o

    h, o = lax.scan(step, h0.astype(jnp.float32), (split(q), split(k), split(v), split(lg)))
    return o.transpose(1, 0, 3, 2, 4).reshape(B, T, H, DV), h


def _gla_bidir(q, k, v, lg_f, lg_b, h_f, h_b):
    o_f, hf = _gla_scan(q, k, v, lg_f, h_f)
    o_b, hb = _gla_scan(q[:, ::-1], k[:, ::-1], v[:, ::-1], lg_b[:, ::-1], h_b)
    return o_f + o_b[:, ::-1], hf, hb


def _gla_out(o, r, norm_g, w_out):
    B, T = o.shape[:2]
    y = _rms(o.astype(r.dtype), norm_g) * jax.nn.silu(r)
    return y.reshape(B, T, GLA_HV) @ w_out


def _ffn(h, w_up, cw, cb, w_down):
    u = _dwconv(h @ w_up, cw, cb)
    g, val = u[..., :D_FF], u[..., D_FF:]
    return (jax.nn.silu(g) * val) @ w_down


def _modulation(cond, w_ada, b_ada):
    m = jax.nn.silu(cond) @ w_ada + b_ada
    return [t[:, None, :] for t in jnp.split(m, 6, axis=-1)]


def _modulate(x, g, shift, scale):
    return _rms(x, g) * (1.0 + scale) + shift


def setup_inputs(seed: int = 0) -> dict:
    key = jax.random.key(seed)
    ks = iter(jax.random.split(key, 40))
    D = D_MODEL

    def nrm(shape, scale):
        return jax.random.normal(next(ks), shape, jnp.float32) * scale

    def gain(shape):
        return 1.0 + nrm(shape, 0.02)

    return {
        'x_prompt': nrm((BATCH, SEQ, D), 1.0),
        'x_sample': nrm((DEC_BATCH, DEC_SEQ, D), 1.0),
        'c': nrm((DEC_BATCH, D), 1.0),
        'cache_k': nrm((DEC_BATCH, N_EVEN, PAST_LEN, N_KV_HEADS, HEAD_DIM), 1.0),
        'cache_v': nrm((DEC_BATCH, N_EVEN, PAST_LEN, N_KV_HEADS, HEAD_DIM), 1.0),
        'state_gla_fwd': nrm((DEC_BATCH, N_ODD, GLA_HEADS, GLA_DK, GLA_DV), 1.0),
        'state_gla_bwd': nrm((DEC_BATCH, N_ODD, GLA_HEADS, GLA_DK, GLA_DV), 1.0),
        'c_ctx': nrm((D,), 1.0),
        'norm1_g': gain((DEPTH, D)),
        'norm2_g': gain((DEPTH, D)),
        'w_ada': nrm((DEPTH, D, 6 * D), 0.5 * D ** -0.5),
        'b_ada': nrm((DEPTH, 6 * D), 0.02),
        'even_w_in': nrm((N_EVEN, D, EVEN_IN), D ** -0.5),
        'conv_w': nrm((N_EVEN, CONV_W, CONV_CH), CONV_W ** -0.5),
        'conv_b': nrm((N_EVEN, CONV_CH), 0.02),
        'conv_ln_g': gain((N_EVEN, CONV_CH)),
        'conv_ln_b': nrm((N_EVEN, CONV_CH), 0.02),
        'q_norm_g': gain((N_EVEN, HEAD_DIM)),
        'k_norm_g': gain((N_EVEN, HEAD_DIM)),
        'even_w_out': nrm((N_EVEN, EVEN_OUT, D), EVEN_OUT ** -0.5),
        'odd_w_in': nrm((N_ODD, D, ODD_IN), D ** -0.5),
        'gla_w_gate': nrm((N_ODD, 2, GLA_RANK, GLA_HK), GLA_RANK ** -0.5),
        'gla_b_gate': nrm((N_ODD, 2, GLA_HK), 0.1),
        'gla_norm_g': gain((N_ODD, GLA_DV)),
        'odd_w_out': nrm((N_ODD, GLA_HV, D), GLA_HV ** -0.5),
        'ffn_w_up': nrm((DEPTH, D, 2 * D_FF), D ** -0.5),
        'ffn_conv_w': nrm((DEPTH, FFN_CONV_W, 2 * D_FF), FFN_CONV_W ** -0.5),
        'ffn_conv_b': nrm((DEPTH, 2 * D_FF), 0.02),
        'ffn_w_down': nrm((DEPTH, D_FF, D), D_FF ** -0.5),
        'final_norm_g': gain((D,)),
    }


def reference(x_prompt, x_sample, c, cache_k, cache_v, state_gla_fwd, state_gla_bwd, c_ctx,
              norm1_g, norm2_g, w_ada, b_ada, even_w_in, conv_w, conv_b, conv_ln_g, conv_ln_b,
              q_norm_g, k_norm_g, even_w_out, odd_w_in, gla_w_gate, gla_b_gate, gla_norm_g,
              odd_w_out, ffn_w_up, ffn_conv_w, ffn_conv_b, ffn_w_down, final_norm_g):
    xp = x_prompt
    xs = x_sample
    Bp = xp.shape[0]
    cos, sin = _axial_angles(xs.shape[1])
    new_k, new_v, new_sf, new_sb = [], [], [], []
    for l in range(DEPTH):
        mc = _modulation(c_ctx[None, :], w_ada[l], b_ada[l])
        ms = _modulation(c, w_ada[l], b_ada[l])
        hp = _modulate(xp, norm1_g[l], mc[0], mc[1])
        hs = _modulate(xs, norm1_g[l], ms[0], ms[1])
        if l % 2 == 0:
            e = l // 2
            ep = (even_w_in[e], conv_w[e], conv_b[e], conv_ln_g[e], conv_ln_b[e], q_norm_g[e], k_norm_g[e])
            ap, qp, kp, vp = _even_proj(hp, *ep)
            op = _attend(qp, kp, vp)
            yp = jnp.concatenate([ap, op], -1) @ even_w_out[e]
            new_k.append(kp)
            new_v.append(vp)
            a_s, qs, ks, vs = _even_proj(hs, *ep)
            qs = _rope(qs, cos, sin)
            ks = _rope(ks, cos, sin)
            k_all = jnp.concatenate([ks, cache_k[:, e].astype(ks.dtype)], 1)
            v_all = jnp.concatenate([vs, cache_v[:, e].astype(vs.dtype)], 1)
            os_ = _attend(qs, k_all, v_all)
            ys = jnp.concatenate([a_s, os_], -1) @ even_w_out[e]
        else:
            o = l // 2
            q, k, v, r, lgf, lgb = _gla_proj(hp, odd_w_in[o], gla_w_gate[o], gla_b_gate[o])
            h0 = jnp.zeros((Bp, GLA_HEADS, GLA_DK, GLA_DV), jnp.float32)
            ob, hf, hb = _gla_bidir(q, k, v, lgf, lgb, h0, h0)
            yp = _gla_out(ob, r, gla_norm_g[o], odd_w_out[o])
            new_sf.append(hf.astype(xp.dtype))
            new_sb.append(hb.astype(xp.dtype))
            q, k, v, r, lgf, lgb = _gla_proj(hs, odd_w_in[o], gla_w_gate[o], gla_b_gate[o])
            ob, _, _ = _gla_bidir(q, k, v, lgf, lgb, state_gla_fwd[:, o], state_gla_bwd[:, o])
            ys = _gla_out(ob, r, gla_norm_g[o], odd_w_out[o])
        xp = xp + mc[2] * yp
        xs = xs + ms[2] * ys
        fp = (ffn_w_up[l], ffn_conv_w[l], ffn_conv_b[l], ffn_w_down[l])
        xp = xp + mc[5] * _ffn(_modulate(xp, norm2_g[l], mc[3], mc[4]), *fp)
        xs = xs + ms[5] * _ffn(_modulate(xs, norm2_g[l], ms[3], ms[4]), *fp)
    y_prompt = _rms(xp, final_norm_g)
    y_sample = _rms(xs, final_norm_g)
    return (y_prompt, y_sample, jnp.stack(new_k, 1), jnp.stack(new_v, 1), jnp.stack(new_sf, 1), jnp.stack(new_sb, 1))
```

```python
import functools

import jax
import jax.numpy as jnp
from jax import lax
from jax.experimental import pallas as pl
from jax.experimental.pallas import tpu as pltpu

F32 = jnp.float32
BF16 = jnp.bfloat16
HIGHEST = lax.Precision.HIGHEST

D_MODEL = 2048
BATCH = 32
SEQ = 256
DEPTH = 4
DEC_BATCH = 4
DEC_SEQ = 2048
PAST_LEN = 512
GRID_W = 64
CONV_CH = D_MODEL // 2
CONV_W = 31
HEAD_DIM = 128
N_Q_HEADS = 8
N_KV_HEADS = 2
GQA_GROUP = N_Q_HEADS // N_KV_HEADS
ROPE_THETA = 10000.0
EVEN_IN = 2 * CONV_CH + (N_Q_HEADS + 2 * N_KV_HEADS) * HEAD_DIM
GLA_HEADS = 4
GLA_DK = 256
GLA_DV = 512
GLA_RANK = 16
GLA_TAU = 16.0
GLA_CHUNK = 64
GLA_HK = GLA_HEADS * GLA_DK
GLA_HV = GLA_HEADS * GLA_DV
GLA_MAIN = 2 * GLA_HK + 2 * GLA_HV
D_FF = 11 * D_MODEL // 4
EPS = 1e-6

NP_ROWS = BATCH * SEQ
NS_ROWS = DEC_BATCH * DEC_SEQ
N_TOK = NP_ROWS + NS_ROWS
N_COND = 8

LANES = 128
MIB = 1024 * 1024


def _cparams(n_axes, vmem_mib):
    return pltpu.CompilerParams(dimension_semantics=("arbitrary",) * n_axes,
                                vmem_limit_bytes=vmem_mib * MIB)


def _modrow(i, tm):
    n_ctx = NP_ROWS // tm
    per_seq = DEC_SEQ // tm
    return jnp.where(i < n_ctx, 0, 1 + (i - n_ctx) // per_seq)


def _silu(x):
    return x * jax.nn.sigmoid(x)


def _ada_kernel(c_ref, w_ref, b_ref, o_ref):
    s = _silu(c_ref[...]).astype(BF16)
    o_ref[...] = jnp.dot(s, w_ref[...].astype(BF16), preferred_element_type=F32) + b_ref[...]


def _ada(cond, w_ada, b_ada):
    tn = 1024
    n6 = 6 * D_MODEL
    return pl.pallas_call(
        _ada_kernel,
        grid=(DEPTH, n6 // tn),
        in_specs=[pl.BlockSpec((N_COND, D_MODEL), lambda l, j: (0, 0)),
                  pl.BlockSpec((None, D_MODEL, tn), lambda l, j: (l, 0, j)),
                  pl.BlockSpec((None, 1, tn), lambda l, j: (l, 0, j))],
        out_specs=pl.BlockSpec((None, N_COND, tn), lambda l, j: (l, 0, j)),
        out_shape=jax.ShapeDtypeStruct((DEPTH, N_COND, n6), F32),
        compiler_params=_cparams(2, 40),
        name="ada",
    )(cond, w_ada, b_ada.reshape(DEPTH, 1, n6))


def _modnorm_kernel(x_ref, g_ref, m_ref, o_ref, *, shift_row, scale_row):
    x = x_ref[...]
    y = x * lax.rsqrt(jnp.mean(x * x, -1, keepdims=True) + EPS) * g_ref[...]
    y = y * (1.0 + m_ref[scale_row:scale_row + 1, :]) + m_ref[shift_row:shift_row + 1, :]
    o_ref[...] = y.astype(o_ref.dtype)


def _modnorm(x, g, mod, shift_row, scale_row):
    tm = 256
    return pl.pallas_call(
        functools.partial(_modnorm_kernel, shift_row=shift_row, scale_row=scale_row),
        grid=(N_TOK // tm,),
        in_specs=[pl.BlockSpec((tm, D_MODEL), lambda i: (i, 0)),
                  pl.BlockSpec((1, D_MODEL), lambda i: (0, 0)),
                  pl.BlockSpec((None, 6, D_MODEL), lambda i: (_modrow(i, tm), 0, 0))],
        out_specs=pl.BlockSpec((tm, D_MODEL), lambda i: (i, 0)),
        out_shape=jax.ShapeDtypeStruct((N_TOK, D_MODEL), BF16),
        compiler_params=_cparams(1, 32),
        name="modnorm",
    )(x, g.reshape(1, D_MODEL), mod)


def _rms_kernel(x_ref, g_ref, o_ref):
    x = x_ref[...]
    o_ref[...] = x * lax.rsqrt(jnp.mean(x * x, -1, keepdims=True) + EPS) * g_ref[...]


def _final_rms(x, g):
    tm = 256
    return pl.pallas_call(
        _rms_kernel,
        grid=(N_TOK // tm,),
        in_specs=[pl.BlockSpec((tm, D_MODEL), lambda i: (i, 0)),
                  pl.BlockSpec((1, D_MODEL), lambda i: (0, 0))],
        out_specs=pl.BlockSpec((tm, D_MODEL), lambda i: (i, 0)),
        out_shape=jax.ShapeDtypeStruct((N_TOK, D_MODEL), F32),
        compiler_params=_cparams(1, 32),
        name="final_rms",
    )(x, g.reshape(1, D_MODEL))


def _mm_kernel(a_ref, w_ref, o_ref):
    o_ref[...] = jnp.dot(a_ref[...], w_ref[...], preferred_element_type=F32).astype(o_ref.dtype)


def _matmul(a, w, out_dtype, tm, tn, vmem_mib, name):
    m, k = a.shape
    n = w.shape[1]
    return pl.pallas_call(
        _mm_kernel,
        grid=(m // tm, n // tn),
        in_specs=[pl.BlockSpec((tm, k), lambda i, j: (i, 0)),
                  pl.BlockSpec((k, tn), lambda i, j: (0, j))],
        out_specs=pl.BlockSpec((tm, tn), lambda i, j: (i, j)),
        out_shape=jax.ShapeDtypeStruct((m, n), out_dtype),
        compiler_params=_cparams(2, vmem_mib),
        name=name,
    )(a, w)


def _mm_res_kernel(*refs, n_lhs, gate_row):
    a_refs = refs[:n_lhs]
    w_refs = refs[n_lhs:2 * n_lhs]
    x_ref, m_ref, o_ref = refs[2 * n_lhs:]
    acc = jnp.dot(a_refs[0][...], w_refs[0][...], preferred_element_type=F32)
    for a_ref, w_ref in zip(a_refs[1:], w_refs[1:]):
        acc = acc + jnp.dot(a_ref[...], w_ref[...], preferred_element_type=F32)
    o_ref[...] = x_ref[...] + m_ref[gate_row:gate_row + 1, :] * acc


def _matmul_residual(lhs, ws, x, mod, gate_row, tm, tn, vmem_mib, name):
    n_lhs = len(lhs)
    n = ws[0].shape[1]
    in_specs = [pl.BlockSpec((tm, a.shape[1]), lambda i, j: (i, 0)) for a in lhs]
    in_specs += [pl.BlockSpec((w.shape[0], tn), lambda i, j: (0, j)) for w in ws]
    in_specs += [pl.BlockSpec((tm, tn), lambda i, j: (i, j)),
                 pl.BlockSpec((None, 6, tn), lambda i, j: (_modrow(i, tm), 0, j))]
    return pl.pallas_call(
        functools.partial(_mm_res_kernel, n_lhs=n_lhs, gate_row=gate_row),
        grid=(N_TOK // tm, n // tn),
        in_specs=in_specs,
        out_specs=pl.BlockSpec((tm, tn), lambda i, j: (i, j)),
        out_shape=jax.ShapeDtypeStruct((N_TOK, n), F32),
        compiler_params=_cparams(2, vmem_mib),
        name=name,
    )(*lhs, *ws, x, mod)


CONV_TM = 256
CONV_HALO = 16


def _convmod_kernel(a_ref, g_ref, ap_ref, gp_ref, an_ref, gn_ref, cw_ref, cb_ref, lng_ref, lnb_ref,
                    o_ref, ext_ref, y_ref):
    tm = CONV_TM
    i = pl.program_id(0)
    n_ctx = NP_ROWS // tm
    per_seq = DEC_SEQ // tm
    j = (i - n_ctx) % per_seq
    latent = i >= n_ctx
    has_prev = jnp.logical_and(latent, j != 0)
    has_next = jnp.logical_and(latent, j != per_seq - 1)

    def glu(a, g):
        return a * jax.nn.sigmoid(g)

    ext_ref[CONV_HALO:CONV_HALO + tm, :] = glu(a_ref[...], g_ref[...])
    ext_ref[0:CONV_HALO, :] = jnp.where(has_prev, glu(ap_ref[...], gp_ref[...]), 0.0)
    ext_ref[CONV_HALO + tm:, :] = jnp.where(has_next, glu(an_ref[...], gn_ref[...]), 0.0)

    off = CONV_HALO - CONV_W // 2
    for cb in range(CONV_CH // LANES):
        cs = slice(cb * LANES, (cb + 1) * LANES)
        acc = jnp.zeros((tm, LANES), F32) + cb_ref[:, cs]
        for k in range(CONV_W):
            acc = acc + cw_ref[k:k + 1, cs] * ext_ref[off + k:off + k + tm, cs]
        y_ref[:, cs] = acc

    y = y_ref[...]
    mu = jnp.mean(y, -1, keepdims=True)
    yc = y - mu
    z = yc * lax.rsqrt(jnp.mean(yc * yc, -1, keepdims=True) + EPS) * lng_ref[...] + lnb_ref[...]
    o_ref[...] = _silu(z).astype(o_ref.dtype)


def _convmod(p, cw, cb, ln_g, ln_b):
    tm = CONV_TM
    hb = tm // CONV_HALO
    last = N_TOK // CONV_HALO - 1
    prev_map = lambda i: (jnp.maximum(i * hb - 1, 0), 0)
    prev_map_g = lambda i: (jnp.maximum(i * hb - 1, 0), 1)
    next_map = lambda i: (jnp.minimum((i + 1) * hb, last), 0)
    next_map_g = lambda i: (jnp.minimum((i + 1) * hb, last), 1)
    vec = pl.BlockSpec((1, CONV_CH), lambda i: (0, 0))
    return pl.pallas_call(
        _convmod_kernel,
        grid=(N_TOK // tm,),
        in_specs=[pl.BlockSpec((tm, CONV_CH), lambda i: (i, 0)),
                  pl.BlockSpec((tm, CONV_CH), lambda i: (i, 1)),
                  pl.BlockSpec((CONV_HALO, CONV_CH), prev_map),
                  pl.BlockSpec((CONV_HALO, CONV_CH), prev_map_g),
                  pl.BlockSpec((CONV_HALO, CONV_CH), next_map),
                  pl.BlockSpec((CONV_HALO, CONV_CH), next_map_g),
                  pl.BlockSpec((CONV_W, CONV_CH), lambda i: (0, 0)),
                  vec, vec, vec],
        out_specs=pl.BlockSpec((tm, CONV_CH), lambda i: (i, 0)),
        out_shape=jax.ShapeDtypeStruct((N_TOK, CONV_CH), BF16),
        scratch_shapes=[pltpu.VMEM((tm + 2 * CONV_HALO, CONV_CH), F32),
                        pltpu.VMEM((tm, CONV_CH), F32)],
        compiler_params=_cparams(1, 32),
        name="convmod",
    )(p, p, p, p, p, p, cw, cb.reshape(1, -1), ln_g.reshape(1, -1), ln_b.reshape(1, -1))


QK_TM = 256


def _qkprep_kernel(q_ref, k_ref, v_ref, cos_ref, sin_ref, qg_ref, kg_ref, qo_ref, kf_ref, kb_ref, vb_ref):
    cos = cos_ref[...]
    sin = sin_ref[...]
    lane = lax.broadcasted_iota(jnp.int32, (QK_TM, HEAD_DIM), 1)
    even = (lane & 1) == 0

    def norm(x, g):
        return x * lax.rsqrt(jnp.mean(x * x, -1, keepdims=True) + EPS) * g

    def rope(x):
        partner = jnp.where(even, pltpu.roll(x, HEAD_DIM - 1, 1), pltpu.roll(x, 1, 1))
        return x * cos + partner * sin

    for h in range(N_Q_HEADS):
        hs = slice(h * HEAD_DIM, (h + 1) * HEAD_DIM)
        qo_ref[:, hs] = (rope(norm(q_ref[:, hs], qg_ref[...])) * HEAD_DIM ** -0.5).astype(qo_ref.dtype)
    for h in range(N_KV_HEADS):
        hs = slice(h * HEAD_DIM, (h + 1) * HEAD_DIM)
        kn = norm(k_ref[:, hs], kg_ref[...])
        kf_ref[:, hs] = kn
        kb_ref[:, hs] = rope(kn).astype(kb_ref.dtype)
    vb_ref[...] = v_ref[...].astype(vb_ref.dtype)


def _qkprep(p, cos_tab, sin_tab, qg, kg):
    tm = QK_TM
    n_ctx = NP_ROWS // tm
    per_seq = DEC_SEQ // tm
    nq = N_Q_HEADS * HEAD_DIM
    nkv = N_KV_HEADS * HEAD_DIM
    q_blk = 2 * CONV_CH // nq
    k_blk = (2 * CONV_CH + nq) // nkv
    tab_map = lambda i: (jnp.where(i < n_ctx, 0, 1 + (i - n_ctx) % per_seq), 0)
    vec = pl.BlockSpec((1, HEAD_DIM), lambda i: (0, 0))
    return pl.pallas_call(
        _qkprep_kernel,
        grid=(N_TOK // tm,),
        in_specs=[pl.BlockSpec((tm, nq), lambda i: (i, q_blk)),
                  pl.BlockSpec((tm, nkv), lambda i: (i, k_blk)),
                  pl.BlockSpec((tm, nkv), lambda i: (i, k_blk + 1)),
                  pl.BlockSpec((tm, HEAD_DIM), tab_map),
                  pl.BlockSpec((tm, HEAD_DIM), tab_map),
                  vec, vec],
        out_specs=[pl.BlockSpec((tm, nq), lambda i: (i, 0)),
                   pl.BlockSpec((tm, nkv), lambda i: (i, 0)),
                   pl.BlockSpec((tm, nkv), lambda i: (i, 0)),
                   pl.BlockSpec((tm, nkv), lambda i: (i, 0))],
        out_shape=[jax.ShapeDtypeStruct((N_TOK, nq), BF16),
                   jax.ShapeDtypeStruct((N_TOK, nkv), F32),
                   jax.ShapeDtypeStruct((N_TOK, nkv), BF16),
                   jax.ShapeDtypeStruct((N_TOK, nkv), BF16)],
        compiler_params=_cparams(1, 32),
        name="qkprep",
    )(p, p, p, cos_tab, sin_tab, qg.reshape(1, -1), kg.reshape(1, -1))


def _rope_tables():
    rows = DEC_SEQ // GRID_W
    row = jnp.repeat(jnp.arange(rows), GRID_W).astype(F32)
    col = jnp.tile(jnp.arange(GRID_W), rows).astype(F32)
    n = HEAD_DIM // 4
    inv = ROPE_THETA ** (-jnp.arange(n, dtype=F32) / n)
    ang = jnp.concatenate([row[:, None] * inv, col[:, None] * inv], -1)
    cos = jnp.repeat(jnp.cos(ang), 2, axis=-1)
    sin = jnp.repeat(jnp.sin(ang), 2, axis=-1) * jnp.tile(jnp.array([-1.0, 1.0], F32), HEAD_DIM // 2)
    cos = jnp.concatenate([jnp.ones((QK_TM, HEAD_DIM), F32), cos], 0)
    sin = jnp.concatenate([jnp.zeros((QK_TM, HEAD_DIM), F32), sin], 0)
    return cos, sin


def _attn_kernel(q_ref, k_ref, v_ref, o_ref):
    k = k_ref[...]
    v = v_ref[...]
    for g in range(GQA_GROUP):
        hs = slice(g * HEAD_DIM, (g + 1) * HEAD_DIM)
        s = lax.dot_general(q_ref[:, hs], k, (((1,), (1,)), ((), ())), preferred_element_type=F32)
        p = jnp.exp(s - jnp.max(s, -1, keepdims=True))
        denom = jnp.sum(p, -1, keepdims=True)
        o = jnp.dot(p.astype(BF16), v, preferred_element_type=F32) / denom
        o_ref[:, hs] = o.astype(o_ref.dtype)


def _attention(q, k, v, tq, name):
    b, t_q, _ = q.shape
    t_k = k.shape[1]
    gw = GQA_GROUP * HEAD_DIM
    return pl.pallas_call(
        _attn_kernel,
        grid=(b, N_KV_HEADS, t_q // tq),
        in_specs=[pl.BlockSpec((None, tq, gw), lambda b_, h, i: (b_, i, h)),
                  pl.BlockSpec((None, t_k, HEAD_DIM), lambda b_, h, i: (b_, 0, h)),
                  pl.BlockSpec((None, t_k, HEAD_DIM), lambda b_, h, i: (b_, 0, h))],
        out_specs=pl.BlockSpec((None, tq, gw), lambda b_, h, i: (b_, i, h)),
        out_shape=jax.ShapeDtypeStruct(q.shape, BF16),
        compiler_params=_cparams(3, 40),
        name=name,
    )(q, k, v)


GATE_TM = 512
GLA_TB = 256


def _gate_kernel(z_ref, wg_ref, bg_ref, o_ref):
    z = z_ref[...]
    for d in range(2):
        zz = jnp.dot(z, wg_ref[d], precision=HIGHEST, preferred_element_type=F32) + bg_ref[d]
        o_ref[d] = (jnp.minimum(zz, 0.0) - jnp.log(1.0 + jnp.exp(-jnp.abs(zz)))) * (1.0 / GLA_TAU)


def _gla_gates(z, wg_pad, bg):
    tm = GATE_TM
    return pl.pallas_call(
        _gate_kernel,
        grid=(N_TOK // tm,),
        in_specs=[pl.BlockSpec((tm, LANES), lambda i: (i, 0)),
                  pl.BlockSpec((2, LANES, GLA_HK), lambda i: (0, 0, 0)),
                  pl.BlockSpec((2, 1, GLA_HK), lambda i: (0, 0, 0))],
        out_specs=pl.BlockSpec((2, tm, GLA_HK), lambda i: (0, i, 0)),
        out_shape=jax.ShapeDtypeStruct((2, N_TOK, GLA_HK), F32),
        compiler_params=_cparams(1, 32),
        name="gla_gate",
    )(z, wg_pad, bg.reshape(2, 1, GLA_HK))


def _gla_kernel(*refs, reverse, has_h0, want_hfin, n_t):
    q_ref, k_ref, v_ref, g_ref = refs[:4]
    rest = list(refs[4:])
    h0_ref = rest.pop(0) if has_h0 else None
    o_ref = rest.pop(0)
    hf_ref = rest.pop(0) if want_hfin else None
    ht_ref = rest.pop(0)
    t = pl.program_id(2)
    L = GLA_CHUNK

    @pl.when(t == 0)
    def _():
        if has_h0:
            ht_ref[...] = h0_ref[...].T
        else:
            ht_ref[...] = jnp.zeros_like(ht_ref)

    r = lax.broadcasted_iota(jnp.int32, (L, L), 0)
    c = lax.broadcasted_iota(jnp.int32, (L, L), 1)
    tri = (r <= c) if reverse else (r >= c)
    tri_f = jnp.where(tri, 1.0, 0.0).astype(F32)
    nt = (((1,), (1,)), ((), ()))
    tn = (((0,), (0,)), ((), ()))
    n_sub = GLA_TB // L
    order = range(n_sub - 1, -1, -1) if reverse else range(n_sub)
    for jj in order:
        rows = slice(jj * L, (jj + 1) * L)
        q = q_ref[rows, :] * GLA_DK ** -0.5
        k = k_ref[rows, :]
        v = v_ref[rows, :].astype(BF16)
        g = g_ref[rows, :]
        b = jnp.dot(tri_f, g, precision=HIGHEST, preferred_element_type=F32)
        qe = (q * jnp.exp(b)).astype(BF16)
        ke = (k * jnp.exp(-b)).astype(BF16)
        a = lax.dot_general(qe, ke, nt, preferred_element_type=F32)
        a = jnp.where(tri, a, 0.0).astype(BF16)
        ht = ht_ref[...]
        o = lax.dot_general(qe, ht.astype(BF16), nt, preferred_element_type=F32)
        o = o + jnp.dot(a, v, preferred_element_type=F32)
        o_ref[rows, :] = o
        bl = b[0:1, :] if reverse else b[L - 1:L, :]
        kd = (k * jnp.exp(bl - b)).astype(BF16)
        ht_ref[...] = ht * jnp.exp(bl) + lax.dot_general(v, kd, tn, preferred_element_type=F32)

    if want_hfin:
        @pl.when(t == n_t - 1)
        def _():
            hf_ref[...] = ht_ref[...].T


def _gla_scan(p, lg, h0, row0, n_b, t_len, reverse, want_hfin, name):
    n_t = t_len // GLA_TB
    blk0 = row0 // GLA_TB
    has_h0 = h0 is not None

    def rowblk(b_, t):
        tt = (n_t - 1 - t) if reverse else t
        return blk0 + b_ * n_t + tt

    k_blk = GLA_HK // GLA_DK
    v_blk = 2 * GLA_HK // GLA_DV
    in_specs = [pl.BlockSpec((GLA_TB, GLA_DK), lambda b_, h, t: (rowblk(b_, t), h)),
                pl.BlockSpec((GLA_TB, GLA_DK), lambda b_, h, t: (rowblk(b_, t), k_blk + h)),
                pl.BlockSpec((GLA_TB, GLA_DV), lambda b_, h, t: (rowblk(b_, t), v_blk + h)),
                pl.BlockSpec((GLA_TB, GLA_DK), lambda b_, h, t: (rowblk(b_, t), h))]
    args = [p, p, p, lg]
    if has_h0:
        in_specs.append(pl.BlockSpec((None, None, GLA_DK, GLA_DV), lambda b_, h, t: (b_, h, 0, 0)))
        args.append(h0)
    o_rows = n_b * t_len
    out_specs = [pl.BlockSpec((GLA_TB, GLA_DV), lambda b_, h, t: (rowblk(b_, t) - blk0, h))]
    out_shape = [jax.ShapeDtypeStruct((o_rows, GLA_HV), F32)]
    if want_hfin:
        out_specs.append(pl.BlockSpec((None, None, GLA_DK, GLA_DV), lambda b_, h, t: (b_, h, 0, 0)))
        out_shape.append(jax.ShapeDtypeStruct((n_b, GLA_HEADS, GLA_DK, GLA_DV), F32))
    return pl.pallas_call(
        functools.partial(_gla_kernel, reverse=reverse, has_h0=has_h0, want_hfin=want_hfin, n_t=n_t),
        grid=(n_b, GLA_HEADS, n_t),
        in_specs=in_specs,
        out_specs=out_specs,
        out_shape=out_shape,
        scratch_shapes=[pltpu.VMEM((GLA_DV, GLA_DK), F32)],
        compiler_params=_cparams(3, 32),
        name=name,
    )(*args)


def _gla_out_kernel(of_ref, ob_ref, r_ref, g_ref, o_ref):
    for h in range(GLA_HEADS):
        hs = slice(h * GLA_DV, (h + 1) * GLA_DV)
        o = of_ref[:, hs] + ob_ref[:, hs]
        y = o * lax.rsqrt(jnp.mean(o * o, -1, keepdims=True) + EPS) * g_ref[...]
        o_ref[:, hs] = (y * _silu(r_ref[:, hs])).astype(o_ref.dtype)


def _gla_out(o_f, o_b, p, norm_g):
    tm = 256
    r_blk = (2 * GLA_HK + GLA_HV) // GLA_HV
    return pl.pallas_call(
        _gla_out_kernel,
        grid=(N_TOK // tm,),
        in_specs=[pl.BlockSpec((tm, GLA_HV), lambda i: (i, 0)),
                  pl.BlockSpec((tm, GLA_HV), lambda i: (i, 0)),
                  pl.BlockSpec((tm, GLA_HV), lambda i: (i, r_blk)),
                  pl.BlockSpec((1, GLA_DV), lambda i: (0, 0))],
        out_specs=pl.BlockSpec((tm, GLA_HV), lambda i: (i, 0)),
        out_shape=jax.ShapeDtypeStruct((N_TOK, GLA_HV), BF16),
        compiler_params=_cparams(1, 32),
        name="gla_out",
    )(o_f, o_b, p, norm_g.reshape(1, GLA_DV))


FFN_TM = 2048
FFN_TN = 256


def _ffn_up_kernel(h_ref, wg_ref, wv_ref, cwg_ref, cwv_ref, cbg_ref, cbv_ref, o_ref):
    tm = FFN_TM
    i = pl.program_id(0)
    seq = jnp.where(i < NP_ROWS // tm, SEQ, DEC_SEQ)
    pos = lax.broadcasted_iota(jnp.int32, (tm, 1), 0) & (seq - 1)
    first = pos == 0
    last = pos == seq - 1
    h = h_ref[...]

    def conv(u, cw_ref, cb_ref):
        up = jnp.where(first, 0.0, pltpu.roll(u, 1, 0))
        un = jnp.where(last, 0.0, pltpu.roll(u, tm - 1, 0))
        return cw_ref[0:1, :] * up + cw_ref[1:2, :] * u + cw_ref[2:3, :] * un + cb_ref[...]

    g = conv(jnp.dot(h, wg_ref[...], preferred_element_type=F32), cwg_ref, cbg_ref)
    val = conv(jnp.dot(h, wv_ref[...], preferred_element_type=F32), cwv_ref, cbv_ref)
    o_ref[...] = (_silu(g) * val).astype(o_ref.dtype)


def _ffn_up(h, w_up, cw, cb):
    tm, tn = FFN_TM, FFN_TN
    nj = D_FF // tn
    return pl.pallas_call(
        _ffn_up_kernel,
        grid=(N_TOK // tm, nj),
        in_specs=[pl.BlockSpec((tm, D_MODEL), lambda i, j: (i, 0)),
                  pl.BlockSpec((D_MODEL, tn), lambda i, j: (0, j)),
                  pl.BlockSpec((D_MODEL, tn), lambda i, j: (0, nj + j)),
                  pl.BlockSpec((3, tn), lambda i, j: (0, j)),
                  pl.BlockSpec((3, tn), lambda i, j: (0, nj + j)),
                  pl.BlockSpec((1, tn), lambda i, j: (0, j)),
                  pl.BlockSpec((1, tn), lambda i, j: (0, nj + j))],
        out_specs=pl.BlockSpec((tm, tn), lambda i, j: (i, j)),
        out_shape=jax.ShapeDtypeStruct((N_TOK, D_FF), BF16),
        compiler_params=_cparams(2, 48),
        name="ffn_up",
    )(h, w_up, w_up, cw, cw, cb.reshape(1, -1), cb.reshape(1, -1))


def kernel(x_prompt, x_sample, c, cache_k, cache_v, state_gla_fwd, state_gla_bwd, c_ctx, norm1_g, norm2_g,
           w_ada, b_ada, even_w_in, conv_w, conv_b, conv_ln_g, conv_ln_b, q_norm_g, k_norm_g, even_w_out,
           odd_w_in, gla_w_gate, gla_b_gate, gla_norm_g, odd_w_out, ffn_w_up, ffn_conv_w, ffn_conv_b,
           ffn_w_down, final_norm_g):
    x = jnp.concatenate([x_prompt.reshape(NP_ROWS, D_MODEL), x_sample.reshape(NS_ROWS, D_MODEL)], 0)
    cond = jnp.concatenate([c_ctx[None, :], c, jnp.zeros((N_COND - 1 - DEC_BATCH, D_MODEL), F32)], 0)
    mod_all = _ada(cond, w_ada, b_ada).reshape(DEPTH, N_COND, 6, D_MODEL)
    cos_tab, sin_tab = _rope_tables()
    nkv = N_KV_HEADS * HEAD_DIM

    new_k, new_v, new_sf, new_sb = [], [], [], []
    for l in range(DEPTH):
        mod = mod_all[l]
        h = _modnorm(x, norm1_g[l], mod, 0, 1)
        if l % 2 == 0:
            e = l // 2
            p = _matmul(h, even_w_in[e].astype(BF16), F32, 1024, 512, 40, "even_in")
            a = _convmod(p, conv_w[e], conv_b[e], conv_ln_g[e], conv_ln_b[e])
            q, k_f32, k_bf, v_bf = _qkprep(p, cos_tab, sin_tab, q_norm_g[e], k_norm_g[e])
            new_k.append(k_f32[:NP_ROWS].reshape(BATCH, SEQ, N_KV_HEADS, HEAD_DIM))
            new_v.append(p[:NP_ROWS, EVEN_IN - nkv:].reshape(BATCH, SEQ, N_KV_HEADS, HEAD_DIM))
            o_p = _attention(q[:NP_ROWS].reshape(BATCH, SEQ, -1),
                             k_bf[:NP_ROWS].reshape(BATCH, SEQ, nkv),
                             v_bf[:NP_ROWS].reshape(BATCH, SEQ, nkv), SEQ, "attn_ctx")
            k_all = jnp.concatenate([k_bf[NP_ROWS:].reshape(DEC_BATCH, DEC_SEQ, nkv),
                                     cache_k[:, e].reshape(DEC_BATCH, PAST_LEN, nkv).astype(BF16)], 1)
            v_all = jnp.concatenate([v_bf[NP_ROWS:].reshape(DEC_BATCH, DEC_SEQ, nkv),
                                     cache_v[:, e].reshape(DEC_BATCH, PAST_LEN, nkv).astype(BF16)], 1)
            o_s = _attention(q[NP_ROWS:].reshape(DEC_BATCH, DEC_SEQ, -1), k_all, v_all, 256, "attn_lat")
            o = jnp.concatenate([o_p.reshape(NP_ROWS, -1), o_s.reshape(NS_ROWS, -1)], 0)
            w_out = even_w_out[e].astype(BF16)
            x = _matmul_residual([a, o], [w_out[:CONV_CH], w_out[CONV_CH:]], x, mod, 2, 1024, 512, 40,
                                 "even_out")
        else:
            od = l // 2
            w_in = odd_w_in[od]
            p = _matmul(h, w_in[:, :GLA_MAIN].astype(BF16), F32, 1024, 512, 40, "odd_in")
            w_z = jnp.pad(w_in[:, GLA_MAIN:], ((0, 0), (0, LANES - 2 * GLA_RANK))).astype(BF16)
            z = _matmul(h, w_z, F32, 1024, LANES, 32, "odd_z")
            wg = gla_w_gate[od]
            wg_pad = jnp.zeros((2, LANES, GLA_HK), F32)
            wg_pad = wg_pad.at[0, :GLA_RANK].set(wg[0]).at[1, GLA_RANK:2 * GLA_RANK].set(wg[1])
            lg = _gla_gates(z, wg_pad, gla_b_gate[od])
            of_p, hf = _gla_scan(p, lg[0], None, 0, BATCH, SEQ, False, True, "gla_ctx_fwd")
            ob_p, hb = _gla_scan(p, lg[1], None, 0, BATCH, SEQ, True, True, "gla_ctx_bwd")
            new_sf.append(hf)
            new_sb.append(hb)
            of_s, = _gla_scan(p, lg[0], state_gla_fwd[:, od], NP_ROWS, DEC_BATCH, DEC_SEQ, False, False,
                              "gla_lat_fwd")
            ob_s, = _gla_scan(p, lg[1], state_gla_bwd[:, od], NP_ROWS, DEC_BATCH, DEC_SEQ, True, False,
                              "gla_lat_bwd")
            y = _gla_out(jnp.concatenate([of_p, of_s], 0), jnp.concatenate([ob_p, ob_s], 0), p,
                         gla_norm_g[od])
            x = _matmul_residual([y], [odd_w_out[od].astype(BF16)], x, mod, 2, 1024, 512, 40, "odd_out")
        h2 = _modnorm(x, norm2_g[l], mod, 3, 4)
        u = _ffn_up(h2, ffn_w_up[l].astype(BF16), ffn_conv_w[l], ffn_conv_b[l])
        x = _matmul_residual([u], [ffn_w_down[l].astype(BF16)], x, mod, 5, 1024, 256, 48, "ffn_down")

    y = _final_rms(x, final_norm_g)
    y_prompt = y[:NP_ROWS].reshape(BATCH, SEQ, D_MODEL)
    y_sample = y[NP_ROWS:].reshape(DEC_BATCH, DEC_SEQ, D_MODEL)
    return (y_prompt, y_sample, jnp.stack(new_k, 1), jnp.stack(new_v, 1), jnp.stack(new_sf, 1),
            jnp.stack(new_sb, 1))
```

```python
import functools

import jax
import jax.numpy as jnp
from jax import lax
from jax.experimental import pallas as pl
from jax.experimental.pallas import tpu as pltpu

F32 = jnp.float32
BF16 = jnp.bfloat16
HIGHEST = lax.Precision.HIGHEST

D_MODEL = 2048
BATCH = 32
SEQ = 256
DEPTH = 4
DEC_BATCH = 4
DEC_SEQ = 2048
PAST_LEN = 512
GRID_W = 64
CONV_CH = D_MODEL // 2
CONV_W = 31
HEAD_DIM = 128
N_Q_HEADS = 8
N_KV_HEADS = 2
GQA_GROUP = N_Q_HEADS // N_KV_HEADS
ROPE_THETA = 10000.0
EVEN_IN = 2 * CONV_CH + (N_Q_HEADS + 2 * N_KV_HEADS) * HEAD_DIM
GLA_HEADS = 4
GLA_DK = 256
GLA_DV = 512
GLA_RANK = 16
GLA_TAU = 16.0
GLA_HK = GLA_HEADS * GLA_DK
GLA_HV = GLA_HEADS * GLA_DV
GLA_MAIN = 2 * GLA_HK + 2 * GLA_HV
D_FF = 11 * D_MODEL // 4
EPS = 1e-6

NP_ROWS = BATCH * SEQ
NS_ROWS = DEC_BATCH * DEC_SEQ
N_TOK = NP_ROWS + NS_ROWS
N_COND = 8

LANES = 128
SUBLANES = 8
MIB = 1024 * 1024


def _cparams(n_axes, vmem_mib):
    return pltpu.CompilerParams(dimension_semantics=("arbitrary",) * n_axes,
                                vmem_limit_bytes=vmem_mib * MIB)


def _modrow(i, tm):
    n_ctx = NP_ROWS // tm
    per_seq = DEC_SEQ // tm
    return jnp.where(i < n_ctx, 0, 1 + (i - n_ctx) // per_seq)


def _silu(x):
    return x * jax.nn.sigmoid(x)


def _ada_kernel(c_ref, w_ref, b_ref, o_ref):
    s = _silu(c_ref[...]).astype(BF16)
    o_ref[...] = jnp.dot(s, w_ref[...].astype(BF16), preferred_element_type=F32) + b_ref[...]


def _ada(cond, w_ada, b_ada):
    tn = 1024
    n6 = 6 * D_MODEL
    return pl.pallas_call(
        _ada_kernel,
        grid=(DEPTH, n6 // tn),
        in_specs=[pl.BlockSpec((N_COND, D_MODEL), lambda l, j: (0, 0)),
                  pl.BlockSpec((None, D_MODEL, tn), lambda l, j: (l, 0, j)),
                  pl.BlockSpec((None, 1, tn), lambda l, j: (l, 0, j))],
        out_specs=pl.BlockSpec((None, N_COND, tn), lambda l, j: (l, 0, j)),
        out_shape=jax.ShapeDtypeStruct((DEPTH, N_COND, n6), F32),
        compiler_params=_cparams(2, 40),
        name="ada",
    )(cond, w_ada, b_ada.reshape(DEPTH, 1, n6))


def _modnorm_kernel(x_ref, g_ref, m_ref, o_ref, *, shift_row, scale_row):
    x = x_ref[...]
    y = x * lax.rsqrt(jnp.mean(x * x, -1, keepdims=True) + EPS) * g_ref[...]
    y = y * (1.0 + m_ref[scale_row:scale_row + 1, :]) + m_ref[shift_row:shift_row + 1, :]
    o_ref[...] = y.astype(o_ref.dtype)


def _modnorm(x, g, mod, shift_row, scale_row):
    tm = 256
    return pl.pallas_call(
        functools.partial(_modnorm_kernel, shift_row=shift_row, scale_row=scale_row),
        grid=(N_TOK // tm,),
        in_specs=[pl.BlockSpec((tm, D_MODEL), lambda i: (i, 0)),
                  pl.BlockSpec((1, D_MODEL), lambda i: (0, 0)),
                  pl.BlockSpec((None, 6, D_MODEL), lambda i: (_modrow(i, tm), 0, 0))],
        out_specs=pl.BlockSpec((tm, D_MODEL), lambda i: (i, 0)),
        out_shape=jax.ShapeDtypeStruct((N_TOK, D_MODEL), BF16),
        compiler_params=_cparams(1, 32),
        name="modnorm",
    )(x, g.reshape(1, D_MODEL), mod)


def _mm_kernel(a_ref, w_ref, o_ref):
    o_ref[...] = jnp.dot(a_ref[...], w_ref[...], preferred_element_type=F32).astype(o_ref.dtype)


def _matmul(a, w, out_dtype, tm, tn, vmem_mib, name):
    m, k = a.shape
    n = w.shape[1]
    return pl.pallas_call(
        _mm_kernel,
        grid=(m // tm, n // tn),
        in_specs=[pl.BlockSpec((tm, k), lambda i, j: (i, 0)),
                  pl.BlockSpec((k, tn), lambda i, j: (0, j))],
        out_specs=pl.BlockSpec((tm, tn), lambda i, j: (i, j)),
        out_shape=jax.ShapeDtypeStruct((m, n), out_dtype),
        compiler_params=_cparams(2, vmem_mib),
        name=name,
    )(a, w)


def _proj_kernel(*refs, n_full, n_split, gate_row, final, tm):
    full_refs = refs[:n_full]
    split_refs = refs[n_full:n_full + 2 * n_split]
    pos = n_full + 2 * n_split
    w_refs = refs[pos:pos + n_full + n_split]
    pos += n_full + n_split
    x_ref, mg_ref, g_ref = refs[pos:pos + 3]
    pos += 3
    if final:
        (y_ref,) = refs[pos:]
    else:
        mn_ref, xo_ref, h_ref = refs[pos:]
    i = pl.program_id(0)
    half = tm // 2

    def body(lhs_refs):
        for r in range(2):
            rows = slice(r * half, (r + 1) * half)
            acc = jnp.dot(lhs_refs[0][rows, :], w_refs[0][...], preferred_element_type=F32)
            for a_ref, w_ref in zip(lhs_refs[1:], w_refs[1:]):
                acc = acc + jnp.dot(a_ref[rows, :], w_ref[...], preferred_element_type=F32)
            xn = x_ref[rows, :] + mg_ref[gate_row:gate_row + 1, :] * acc
            y = xn * lax.rsqrt(jnp.mean(xn * xn, -1, keepdims=True) + EPS) * g_ref[...]
            if final:
                y_ref[rows, :] = y
            else:
                xo_ref[rows, :] = xn
                h_ref[rows, :] = (y * (1.0 + mn_ref[1:2, :]) + mn_ref[0:1, :]).astype(h_ref.dtype)

    if n_split:
        n_ctx = NP_ROWS // tm

        @pl.when(i < n_ctx)
        def _():
            body(list(full_refs) + list(split_refs[0::2]))

        @pl.when(i >= n_ctx)
        def _():
            body(list(full_refs) + list(split_refs[1::2]))
    else:
        body(list(full_refs))


def _proj_res_norm(full, split, ws, x, mod_gate, gate_row, norm_g, mod_next, tm, vmem_mib, name):
    final = mod_next is None
    n_ctx = NP_ROWS // tm
    once = pl.Buffered(1)
    in_specs = [pl.BlockSpec((tm, a.shape[1]), lambda i: (i, 0)) for a in full]
    args = list(full)
    for a_ctx, a_lat in split:
        in_specs.append(pl.BlockSpec((tm, a_ctx.shape[1]), lambda i: (jnp.minimum(i, n_ctx - 1), 0)))
        in_specs.append(pl.BlockSpec((tm, a_lat.shape[1]), lambda i: (jnp.maximum(i - n_ctx, 0), 0)))
        args += [a_ctx, a_lat]
    in_specs += [pl.BlockSpec(w.shape, lambda i: (0, 0), pipeline_mode=once) for w in ws]
    args += list(ws)
    row_spec = pl.BlockSpec((tm, D_MODEL), lambda i: (i, 0))
    in_specs += [row_spec,
                 pl.BlockSpec((None, 6, D_MODEL), lambda i: (_modrow(i, tm), 0, 0)),
                 pl.BlockSpec((1, D_MODEL), lambda i: (0, 0))]
    args += [x, mod_gate, norm_g.reshape(1, D_MODEL)]
    if final:
        out_specs = row_spec
        out_shape = jax.ShapeDtypeStruct((N_TOK, D_MODEL), F32)
    else:
        in_specs.append(pl.BlockSpec((None, 2, D_MODEL), lambda i: (_modrow(i, tm), 0, 0)))
        args.append(mod_next)
        out_specs = [row_spec, row_spec]
        out_shape = [jax.ShapeDtypeStruct((N_TOK, D_MODEL), F32),
                     jax.ShapeDtypeStruct((N_TOK, D_MODEL), BF16)]
    return pl.pallas_call(
        functools.partial(_proj_kernel, n_full=len(full), n_split=len(split), gate_row=gate_row,
                          final=final, tm=tm),
        grid=(N_TOK // tm,),
        in_specs=in_specs,
        out_specs=out_specs,
        out_shape=out_shape,
        compiler_params=_cparams(1, vmem_mib),
        name=name,
    )(*args)


CONV_TM = 256
CONV_HALO = 16


def _convmod_kernel(a_ref, g_ref, ap_ref, gp_ref, an_ref, gn_ref, cw_ref, cb_ref, lng_ref, lnb_ref,
                    o_ref, ext_ref, y_ref, win_ref):
    tm = CONV_TM
    i = pl.program_id(0)
    n_ctx = NP_ROWS // tm
    per_seq = DEC_SEQ // tm
    j = (i - n_ctx) % per_seq
    latent = i >= n_ctx
    has_prev = jnp.logical_and(latent, j != 0)
    has_next = jnp.logical_and(latent, j != per_seq - 1)

    def glu(a, g):
        return a * jax.nn.sigmoid(g)

    ext_ref[CONV_HALO:CONV_HALO + tm, :] = glu(a_ref[...], g_ref[...])
    ext_ref[0:CONV_HALO, :] = jnp.where(has_prev, glu(ap_ref[...], gp_ref[...]), 0.0)
    ext_ref[CONV_HALO + tm:, :] = jnp.where(has_next, glu(an_ref[...], gn_ref[...]), 0.0)

    off = CONV_HALO - CONV_W // 2
    span = tm + SUBLANES * ((off + CONV_W - 1) // SUBLANES)
    hrows = tm // 2
    for cb in range(CONV_CH // LANES):
        cs = slice(cb * LANES, (cb + 1) * LANES)
        for r in range(SUBLANES):
            win_ref[r, 0:span, :] = ext_ref[r:r + span, cs]
        for base in range(0, tm, hrows):
            acc = jnp.zeros((hrows, LANES), F32) + cb_ref[:, cs]
            for k in range(CONV_W):
                r = (off + k) % SUBLANES
                m = (off + k) // SUBLANES * SUBLANES
                acc = acc + cw_ref[k:k + 1, cs] * win_ref[r, base + m:base + m + hrows, :]
            y_ref[base:base + hrows, cs] = acc

    y = y_ref[...]
    mu = jnp.mean(y, -1, keepdims=True)
    yc = y - mu
    z = yc * lax.rsqrt(jnp.mean(yc * yc, -1, keepdims=True) + EPS) * lng_ref[...] + lnb_ref[...]
    o_ref[...] = _silu(z).astype(o_ref.dtype)


def _convmod(p, cw, cb, ln_g, ln_b):
    tm = CONV_TM
    hb = tm // CONV_HALO
    last = N_TOK // CONV_HALO - 1
    prev_map = lambda i: (jnp.maximum(i * hb - 1, 0), 0)
    prev_map_g = lambda i: (jnp.maximum(i * hb - 1, 0), 1)
    next_map = lambda i: (jnp.minimum((i + 1) * hb, last), 0)
    next_map_g = lambda i: (jnp.minimum((i + 1) * hb, last), 1)
    vec = pl.BlockSpec((1, CONV_CH), lambda i: (0, 0))
    return pl.pallas_call(
        _convmod_kernel,
        grid=(N_TOK // tm,),
        in_specs=[pl.BlockSpec((tm, CONV_CH), lambda i: (i, 0)),
                  pl.BlockSpec((tm, CONV_CH), lambda i: (i, 1)),
                  pl.BlockSpec((CONV_HALO, CONV_CH), prev_map),
                  pl.BlockSpec((CONV_HALO, CONV_CH), prev_map_g),
                  pl.BlockSpec((CONV_HALO, CONV_CH), next_map),
                  pl.BlockSpec((CONV_HALO, CONV_CH), next_map_g),
                  pl.BlockSpec((CONV_W, CONV_CH), lambda i: (0, 0)),
                  vec, vec, vec],
        out_specs=pl.BlockSpec((tm, CONV_CH), lambda i: (i, 0)),
        out_shape=jax.ShapeDtypeStruct((N_TOK, CONV_CH), BF16),
        scratch_shapes=[pltpu.VMEM((tm + 2 * CONV_HALO, CONV_CH), F32),
                        pltpu.VMEM((tm, CONV_CH), F32),
                        pltpu.VMEM((SUBLANES, tm + 2 * CONV_HALO, LANES), F32)],
        compiler_params=_cparams(1, 32),
        name="convmod",
    )(p, p, p, p, p, p, cw, cb.reshape(1, -1), ln_g.reshape(1, -1), ln_b.reshape(1, -1))


ATT_GW = GQA_GROUP * HEAD_DIM
ATT_Q_BLK = 2 * CONV_CH // ATT_GW
ATT_K_BLK = (2 * CONV_CH + N_Q_HEADS * HEAD_DIM) // HEAD_DIM
ATT_V_BLK = ATT_K_BLK + N_KV_HEADS
ATT_TQ = 256


def _rms_rows(x, g):
    return x * lax.rsqrt(jnp.mean(x * x, -1, keepdims=True) + EPS) * g


def _rope(x, cos, sin):
    lane = lax.broadcasted_iota(jnp.int32, x.shape, 1)
    partner = jnp.where((lane & 1) == 0, pltpu.roll(x, HEAD_DIM - 1, 1), pltpu.roll(x, 1, 1))
    return x * cos + partner * sin


def _attend_group(q_ref, qg_ref, rotate, k, v, o_ref):
    for g in range(GQA_GROUP):
        hs = slice(g * HEAD_DIM, (g + 1) * HEAD_DIM)
        q = _rms_rows(q_ref[:, hs], qg_ref[...])
        if rotate is not None:
            q = rotate(q)
        q = (q * HEAD_DIM ** -0.5).astype(BF16)
        s = lax.dot_general(q, k, (((1,), (1,)), ((), ())), preferred_element_type=F32)
        p = jnp.exp(s - jnp.max(s, -1, keepdims=True))
        denom = jnp.sum(p, -1, keepdims=True)
        o = jnp.dot(p.astype(BF16), v, preferred_element_type=F32) / denom
        o_ref[:, hs] = o.astype(o_ref.dtype)


def _attn_ctx_kernel(q_ref, k_ref, v_ref, qg_ref, kg_ref, o_ref, kf_ref, vf_ref):
    kn = _rms_rows(k_ref[...], kg_ref[...])
    v = v_ref[...]
    kf_ref[...] = kn
    vf_ref[...] = v
    _attend_group(q_ref, qg_ref, None, kn.astype(BF16), v.astype(BF16), o_ref)


def _attn_ctx(p, qg, kg):
    nkv = N_KV_HEADS * HEAD_DIM
    vec = pl.BlockSpec((1, HEAD_DIM), lambda b, h: (0, 0))
    kv_out = pl.BlockSpec((SEQ, HEAD_DIM), lambda b, h: (b, h))
    return pl.pallas_call(
        _attn_ctx_kernel,
        grid=(BATCH, N_KV_HEADS),
        in_specs=[pl.BlockSpec((SEQ, ATT_GW), lambda b, h: (b, ATT_Q_BLK + h)),
                  pl.BlockSpec((SEQ, HEAD_DIM), lambda b, h: (b, ATT_K_BLK + h)),
                  pl.BlockSpec((SEQ, HEAD_DIM), lambda b, h: (b, ATT_V_BLK + h)),
                  vec, vec],
        out_specs=[pl.BlockSpec((SEQ, ATT_GW), lambda b, h: (b, h)), kv_out, kv_out],
        out_shape=[jax.ShapeDtypeStruct((NP_ROWS, N_Q_HEADS * HEAD_DIM), BF16),
                   jax.ShapeDtypeStruct((NP_ROWS, nkv), F32),
                   jax.ShapeDtypeStruct((NP_ROWS, nkv), F32)],
        compiler_params=_cparams(2, 32),
        name="attn_ctx",
    )(p, p, p, qg.reshape(1, -1), kg.reshape(1, -1))


def _attn_lat_kernel(q_ref, k_ref, v_ref, ck_ref, cv_ref, cosq_ref, sinq_ref, cosk_ref, sink_ref,
                     qg_ref, kg_ref, o_ref, ks_ref, vs_ref):
    @pl.when(pl.program_id(2) == 0)
    def _():
        for r0 in range(0, DEC_SEQ, ATT_TQ):
            rows = slice(r0, r0 + ATT_TQ)
            kn = _rms_rows(k_ref[rows, :], kg_ref[...])
            ks_ref[rows, :] = _rope(kn, cosk_ref[rows, :], sink_ref[rows, :]).astype(BF16)
            vs_ref[rows, :] = v_ref[rows, :].astype(BF16)
        ks_ref[DEC_SEQ:, :] = ck_ref[...].astype(BF16)
        vs_ref[DEC_SEQ:, :] = cv_ref[...].astype(BF16)

    cos = cosq_ref[...]
    sin = sinq_ref[...]
    _attend_group(q_ref, qg_ref, lambda x: _rope(x, cos, sin), ks_ref[...], vs_ref[...], o_ref)


def _attn_lat(p, cache_k, cache_v, cos_tab, sin_tab, qg, kg):
    tq = ATT_TQ
    n_q = DEC_SEQ // tq
    row0 = NP_ROWS // tq
    seq0 = NP_ROWS // DEC_SEQ
    vec = pl.BlockSpec((1, HEAD_DIM), lambda b, h, i: (0, 0))
    cache = pl.BlockSpec((None, PAST_LEN, HEAD_DIM), lambda b, h, i: (b, 0, h))
    tab_q = pl.BlockSpec((tq, HEAD_DIM), lambda b, h, i: (i, 0))
    tab_k = pl.BlockSpec((DEC_SEQ, HEAD_DIM), lambda b, h, i: (0, 0))
    return pl.pallas_call(
        _attn_lat_kernel,
        grid=(DEC_BATCH, N_KV_HEADS, n_q),
        in_specs=[pl.BlockSpec((tq, ATT_GW), lambda b, h, i: (row0 + b * n_q + i, ATT_Q_BLK + h)),
                  pl.BlockSpec((DEC_SEQ, HEAD_DIM), lambda b, h, i: (seq0 + b, ATT_K_BLK + h)),
                  pl.BlockSpec((DEC_SEQ, HEAD_DIM), lambda b, h, i: (seq0 + b, ATT_V_BLK + h)),
                  cache, cache, tab_q, tab_q, tab_k, tab_k, vec, vec],
        out_specs=pl.BlockSpec((tq, ATT_GW), lambda b, h, i: (b * n_q + i, h)),
        out_shape=jax.ShapeDtypeStruct((NS_ROWS, N_Q_HEADS * HEAD_DIM), BF16),
        scratch_shapes=[pltpu.VMEM((DEC_SEQ + PAST_LEN, HEAD_DIM), BF16),
                        pltpu.VMEM((DEC_SEQ + PAST_LEN, HEAD_DIM), BF16)],
        compiler_params=_cparams(3, 40),
        name="attn_lat",
    )(p, p, p, cache_k, cache_v, cos_tab, sin_tab, cos_tab, sin_tab, qg.reshape(1, -1), kg.reshape(1, -1))


def _rope_tables():
    rows = DEC_SEQ // GRID_W
    row = jnp.repeat(jnp.arange(rows), GRID_W).astype(F32)
    col = jnp.tile(jnp.arange(GRID_W), rows).astype(F32)
    n = HEAD_DIM // 4
    inv = ROPE_THETA ** (-jnp.arange(n, dtype=F32) / n)
    ang = jnp.concatenate([row[:, None] * inv, col[:, None] * inv], -1)
    cos = jnp.repeat(jnp.cos(ang), 2, axis=-1)
    sin = jnp.repeat(jnp.sin(ang), 2, axis=-1) * jnp.tile(jnp.array([-1.0, 1.0], F32), HEAD_DIM // 2)
    return cos, sin


GLA_CH = 128
GATE_TM = 512
GLA_TB = 256


def _gate_kernel(z_ref, wg_ref, bg_ref, o_ref):
    z = z_ref[...]
    r = lax.broadcasted_iota(jnp.int32, (GLA_CH, GLA_CH), 0)
    c = lax.broadcasted_iota(jnp.int32, (GLA_CH, GLA_CH), 1)
    for d in range(2):
        zz = jnp.dot(z, wg_ref[d], precision=HIGHEST, preferred_element_type=F32) + bg_ref[d]
        lg = (jnp.minimum(zz, 0.0) - jnp.log(1.0 + jnp.exp(-jnp.abs(zz)))) * (1.0 / GLA_TAU)
        hi = lg.astype(BF16)
        rest = lg - hi.astype(F32)
        mid = rest.astype(BF16)
        lo = (rest - mid.astype(F32)).astype(BF16)
        tri = jnp.where((r >= c) if d == 0 else (r <= c), 1.0, 0.0).astype(BF16)
        for r0 in range(0, GATE_TM, GLA_CH):
            rows = slice(r0, r0 + GLA_CH)
            o_ref[d, rows, :] = (jnp.dot(tri, hi[rows, :], preferred_element_type=F32)
                                 + jnp.dot(tri, mid[rows, :], preferred_element_type=F32)
                                 + jnp.dot(tri, lo[rows, :], preferred_element_type=F32))


def _gla_gates(z, wg_pad, bg):
    tm = GATE_TM
    return pl.pallas_call(
        _gate_kernel,
        grid=(N_TOK // tm,),
        in_specs=[pl.BlockSpec((tm, LANES), lambda i: (i, 0)),
                  pl.BlockSpec((2, LANES, GLA_HK), lambda i: (0, 0, 0)),
                  pl.BlockSpec((2, 1, GLA_HK), lambda i: (0, 0, 0))],
        out_specs=pl.BlockSpec((2, tm, GLA_HK), lambda i: (0, i, 0)),
        out_shape=jax.ShapeDtypeStruct((2, N_TOK, GLA_HK), F32),
        compiler_params=_cparams(1, 32),
        name="gla_gate",
    )(z, wg_pad, bg.reshape(2, 1, GLA_HK))


def _gla_kernel(*refs, n_t, t_len, has_h0, want_hfin):
    qf_ref, kf_ref, vf_ref, bf_ref, qb_ref, kb_ref, vb_ref, bb_ref, r_ref, gn_ref = refs[:10]
    rest = list(refs[10:])
    h0f_ref = rest.pop(0) if has_h0 else None
    h0b_ref = rest.pop(0) if has_h0 else None
    y_ref = rest.pop(0)
    hff_ref = rest.pop(0) if want_hfin else None
    hfb_ref = rest.pop(0) if want_hfin else None
    htf_ref, htb_ref, o_ref = rest
    t = pl.program_id(2)
    L = GLA_CH
    mid = L // 2

    @pl.when(t == 0)
    def _():
        if has_h0:
            htf_ref[...] = h0f_ref[...].T
            htb_ref[...] = h0b_ref[...].T
        else:
            htf_ref[...] = jnp.zeros_like(htf_ref)
            htb_ref[...] = jnp.zeros_like(htb_ref)
        o_ref[...] = jnp.zeros_like(o_ref)

    row = lax.broadcasted_iota(jnp.int32, (L, L), 0)
    col = lax.broadcasted_iota(jnp.int32, (L, L), 1)
    nt = (((1,), (1,)), ((), ()))
    tn = (((0,), (0,)), ((), ()))

    def chunk(q_ref, k_ref, v_ref, b_ref, ht_ref, r0, out0, reverse):
        rows = slice(r0, r0 + L)
        q = q_ref[rows, :] * GLA_DK ** -0.5
        k = k_ref[rows, :]
        v = v_ref[rows, :].astype(BF16)
        b = b_ref[rows, :]
        b_end = b[0:1, :] if reverse else b[L - 1:L, :]
        b_mid = b[mid:mid + 1, :] if reverse else b[mid - 1:mid, :]
        qe = (q * jnp.exp(b - b_mid)).astype(BF16)
        ke = (k * jnp.exp(b_mid - b)).astype(BF16)
        a = lax.dot_general(qe, ke, nt, preferred_element_type=F32)
        a = jnp.where((row <= col) if reverse else (row >= col), a, 0.0).astype(BF16)
        ht = ht_ref[...]
        o = lax.dot_general((q * jnp.exp(b)).astype(BF16), ht.astype(BF16), nt, preferred_element_type=F32)
        o = o + jnp.dot(a, v, preferred_element_type=F32)
        o_ref[pl.ds(out0, L), :] += o
        kd = (k * jnp.exp(b_end - b)).astype(BF16)
        ht_ref[...] = ht * jnp.exp(b_end) + lax.dot_general(v, kd, tn, preferred_element_type=F32)

    n_c = GLA_TB // L
    for c in range(n_c):
        chunk(qf_ref, kf_ref, vf_ref, bf_ref, htf_ref, c * L,
              pl.multiple_of(t * GLA_TB + c * L, L), False)
        cb = n_c - 1 - c
        chunk(qb_ref, kb_ref, vb_ref, bb_ref, htb_ref, cb * L,
              pl.multiple_of((n_t - 1 - t) * GLA_TB + cb * L, L), True)

    @pl.when(t == n_t - 1)
    def _():
        for r0 in range(0, t_len, GLA_TB):
            rows = slice(r0, r0 + GLA_TB)
            o = o_ref[rows, :]
            y = o * lax.rsqrt(jnp.mean(o * o, -1, keepdims=True) + EPS) * gn_ref[...]
            y_ref[rows, :] = (y * _silu(r_ref[rows, :])).astype(y_ref.dtype)
        if want_hfin:
            hff_ref[...] = htf_ref[...].T
            hfb_ref[...] = htb_ref[...].T


def _gla(p, bcum, norm_g, h0f, h0b, row0, n_b, t_len, want_hfin, name):
    n_t = t_len // GLA_TB
    blk0 = row0 // GLA_TB
    seq0 = row0 // t_len
    has_h0 = h0f is not None
    k_blk = GLA_HK // GLA_DK
    v_blk = 2 * GLA_HK // GLA_DV
    r_blk = (2 * GLA_HK + GLA_HV) // GLA_DV

    def fwd(b_, t):
        return blk0 + b_ * n_t + t

    def bwd(b_, t):
        return blk0 + b_ * n_t + (n_t - 1 - t)

    def specs(rb):
        return [pl.BlockSpec((GLA_TB, GLA_DK), lambda b_, h, t: (rb(b_, t), h)),
                pl.BlockSpec((GLA_TB, GLA_DK), lambda b_, h, t: (rb(b_, t), k_blk + h)),
                pl.BlockSpec((GLA_TB, GLA_DV), lambda b_, h, t: (rb(b_, t), v_blk + h)),
                pl.BlockSpec((GLA_TB, GLA_DK), lambda b_, h, t: (rb(b_, t), h))]

    state = pl.BlockSpec((None, None, GLA_DK, GLA_DV), lambda b_, h, t: (b_, h, 0, 0))
    in_specs = specs(fwd) + specs(bwd) + [
        pl.BlockSpec((t_len, GLA_DV), lambda b_, h, t: (seq0 + b_, r_blk + h)),
        pl.BlockSpec((1, GLA_DV), lambda b_, h, t: (0, 0))]
    args = [p, p, p, bcum[0], p, p, p, bcum[1], p, norm_g.reshape(1, GLA_DV)]
    if has_h0:
        in_specs += [state, state]
        args += [h0f, h0b]
    out_specs = [pl.BlockSpec((t_len, GLA_DV), lambda b_, h, t: (b_, h))]
    out_shape = [jax.ShapeDtypeStruct((n_b * t_len, GLA_HV), BF16)]
    if want_hfin:
        out_specs += [state, state]
        out_shape += [jax.ShapeDtypeStruct((n_b, GLA_HEADS, GLA_DK, GLA_DV), F32)] * 2
    return pl.pallas_call(
        functools.partial(_gla_kernel, n_t=n_t, t_len=t_len, has_h0=has_h0, want_hfin=want_hfin),
        grid=(n_b, GLA_HEADS, n_t),
        in_specs=in_specs,
        out_specs=out_specs,
        out_shape=out_shape,
        scratch_shapes=[pltpu.VMEM((GLA_DV, GLA_DK), F32), pltpu.VMEM((GLA_DV, GLA_DK), F32),
                        pltpu.VMEM((t_len, GLA_DV), F32)],
        compiler_params=_cparams(3, 40),
        name=name,
    )(*args)


FFN_TM = 2048
FFN_TN = 256


def _ffn_up_kernel(h_ref, wg_ref, wv_ref, cwg_ref, cwv_ref, cbg_ref, cbv_ref, o_ref):
    tm = FFN_TM
    i = pl.program_id(0)
    seq = jnp.where(i < NP_ROWS // tm, SEQ, DEC_SEQ)
    sub = lax.broadcasted_iota(jnp.int32, (SUBLANES, FFN_TN), 0)
    h = h_ref[...]

    def zero_at_edges(x, first):
        pieces = []
        for t in range(0, tm, SEQ):
            seg = x[t:t + SEQ]
            if first:
                at_edge = (t & (seq - 1)) == 0
                pieces += [jnp.where(at_edge & (sub == 0), 0.0, seg[:SUBLANES]), seg[SUBLANES:]]
            else:
                at_edge = ((t + SEQ) & (seq - 1)) == 0
                pieces += [seg[:SEQ - SUBLANES],
                           jnp.where(at_edge & (sub == SUBLANES - 1), 0.0, seg[SEQ - SUBLANES:])]
        return jnp.concatenate(pieces, 0)

    def conv(u, cw_ref, cb_ref):
        up = zero_at_edges(pltpu.roll(u, 1, 0), True)
        un = zero_at_edges(pltpu.roll(u, tm - 1, 0), False)
        return cw_ref[0:1, :] * up + cw_ref[1:2, :] * u + cw_ref[2:3, :] * un + cb_ref[...]

    g = conv(jnp.dot(h, wg_ref[...], preferred_element_type=F32), cwg_ref, cbg_ref)
    val = conv(jnp.dot(h, wv_ref[...], preferred_element_type=F32), cwv_ref, cbv_ref)
    o_ref[...] = (_silu(g) * val).astype(o_ref.dtype)


def _ffn_up(h, w_up, cw, cb):
    tm, tn = FFN_TM, FFN_TN
    nj = D_FF // tn
    return pl.pallas_call(
        _ffn_up_kernel,
        grid=(N_TOK // tm, nj),
        in_specs=[pl.BlockSpec((tm, D_MODEL), lambda i, j: (i, 0)),
                  pl.BlockSpec((D_MODEL, tn), lambda i, j: (0, j)),
                  pl.BlockSpec((D_MODEL, tn), lambda i, j: (0, nj + j)),
                  pl.BlockSpec((3, tn), lambda i, j: (0, j)),
                  pl.BlockSpec((3, tn), lambda i, j: (0, nj + j)),
                  pl.BlockSpec((1, tn), lambda i, j: (0, j)),
                  pl.BlockSpec((1, tn), lambda i, j: (0, nj + j))],
        out_specs=pl.BlockSpec((tm, tn), lambda i, j: (i, j)),
        out_shape=jax.ShapeDtypeStruct((N_TOK, D_FF), BF16),
        compiler_params=_cparams(2, 48),
        name="ffn_up",
    )(h, w_up, w_up, cw, cw, cb.reshape(1, -1), cb.reshape(1, -1))


def kernel(x_prompt, x_sample, c, cache_k, cache_v, state_gla_fwd, state_gla_bwd, c_ctx, norm1_g, norm2_g,
           w_ada, b_ada, even_w_in, conv_w, conv_b, conv_ln_g, conv_ln_b, q_norm_g, k_norm_g, even_w_out,
           odd_w_in, gla_w_gate, gla_b_gate, gla_norm_g, odd_w_out, ffn_w_up, ffn_conv_w, ffn_conv_b,
           ffn_w_down, final_norm_g):
    x = jnp.concatenate([x_prompt.reshape(NP_ROWS, D_MODEL), x_sample.reshape(NS_ROWS, D_MODEL)], 0)
    cond = jnp.concatenate([c_ctx[None, :], c, jnp.zeros((N_COND - 1 - DEC_BATCH, D_MODEL), F32)], 0)
    mod_all = _ada(cond, w_ada, b_ada).reshape(DEPTH, N_COND, 6, D_MODEL)
    cos_tab, sin_tab = _rope_tables()
    nkv = N_KV_HEADS * HEAD_DIM

    new_k, new_v, new_sf, new_sb = [], [], [], []
    h = _modnorm(x, norm1_g[0], mod_all[0], 0, 1)
    for l in range(DEPTH):
        mod = mod_all[l]
        mod_norm2 = mod[:, 3:5]
        if l % 2 == 0:
            e = l // 2
            p = _matmul(h, even_w_in[e].astype(BF16), F32, 1024, EVEN_IN // 2, 48, "even_in")
            a = _convmod(p, conv_w[e], conv_b[e], conv_ln_g[e], conv_ln_b[e])
            o_p, k_new, v_new = _attn_ctx(p, q_norm_g[e], k_norm_g[e])
            new_k.append(k_new.reshape(BATCH, SEQ, N_KV_HEADS, HEAD_DIM))
            new_v.append(v_new.reshape(BATCH, SEQ, N_KV_HEADS, HEAD_DIM))
            o_s = _attn_lat(p, cache_k[:, e].reshape(DEC_BATCH, PAST_LEN, nkv),
                            cache_v[:, e].reshape(DEC_BATCH, PAST_LEN, nkv), cos_tab, sin_tab,
                            q_norm_g[e], k_norm_g[e])
            w_out = even_w_out[e].astype(BF16)
            x, h2 = _proj_res_norm([a], [(o_p, o_s)], [w_out[:CONV_CH], w_out[CONV_CH:]], x, mod, 2,
                                   norm2_g[l], mod_norm2, 512, 48, "even_out")
        else:
            od = l // 2
            w_in = odd_w_in[od]
            p = _matmul(h, w_in[:, :GLA_MAIN].astype(BF16), F32, 1024, 1024, 40, "odd_in")
            w_z = jnp.pad(w_in[:, GLA_MAIN:], ((0, 0), (0, LANES - 2 * GLA_RANK))).astype(BF16)
            z = _matmul(h, w_z, F32, 1024, LANES, 32, "odd_z")
            wg = gla_w_gate[od]
            wg_pad = jnp.zeros((2, LANES, GLA_HK), F32)
            wg_pad = wg_pad.at[0, :GLA_RANK].set(wg[0]).at[1, GLA_RANK:2 * GLA_RANK].set(wg[1])
            bcum = _gla_gates(z, wg_pad, gla_b_gate[od])
            y_p, hf, hb = _gla(p, bcum, gla_norm_g[od], None, None, 0, BATCH, SEQ, True, "gla_ctx")
            new_sf.append(hf)
            new_sb.append(hb)
            (y_s,) = _gla(p, bcum, gla_norm_g[od], state_gla_fwd[:, od], state_gla_bwd[:, od], NP_ROWS,
                          DEC_BATCH, DEC_SEQ, False, "gla_lat")
            x, h2 = _proj_res_norm([], [(y_p, y_s)], [odd_w_out[od].astype(BF16)], x, mod, 2, norm2_g[l],
                                   mod_norm2, 512, 48, "odd_out")
        u = _ffn_up(h2, ffn_w_up[l].astype(BF16), ffn_conv_w[l], ffn_conv_b[l])
        w_down = ffn_w_down[l].astype(BF16)
        if l + 1 < DEPTH:
            x, h = _proj_res_norm([u], [], [w_down], x, mod, 5, norm1_g[l + 1], mod_all[l + 1][:, 0:2],
                                  256, 56, "ffn_down")
        else:
            y = _proj_res_norm([u], [], [w_down], x, mod, 5, final_norm_g, None, 256, 56, "ffn_down_final")

    y_prompt = y[:NP_ROWS].reshape(BATCH, SEQ, D_MODEL)
    y_sample = y[NP_ROWS:].reshape(DEC_BATCH, DEC_SEQ, D_MODEL)
    return (y_prompt, y_sample, jnp.stack(new_k, 1), jnp.stack(new_v, 1), jnp.stack(new_sf, 1),
            jnp.stack(new_sb, 1))
```

```python
import functools

import jax
import jax.numpy as jnp
from jax import lax
from jax.experimental import pallas as pl
from jax.experimental.pallas import tpu as pltpu

F32 = jnp.float32
BF16 = jnp.bfloat16

D_MODEL = 2048
BATCH = 32
SEQ = 256
DEPTH = 4
DEC_BATCH = 4
DEC_SEQ = 2048
PAST_LEN = 512
GRID_W = 64
CONV_CH = D_MODEL // 2
CONV_W = 31
HEAD_DIM = 128
N_Q_HEADS = 8
N_KV_HEADS = 2
GQA_GROUP = N_Q_HEADS // N_KV_HEADS
ROPE_THETA = 10000.0
EVEN_IN = 2 * CONV_CH + (N_Q_HEADS + 2 * N_KV_HEADS) * HEAD_DIM
GLA_HEADS = 4
GLA_DK = 256
GLA_DV = 512
GLA_RANK = 16
GLA_TAU = 16.0
GLA_HK = GLA_HEADS * GLA_DK
GLA_HV = GLA_HEADS * GLA_DV
GLA_MAIN = 2 * GLA_HK + 2 * GLA_HV
D_FF = 11 * D_MODEL // 4
EPS = 1e-6

NP_ROWS = BATCH * SEQ
NS_ROWS = DEC_BATCH * DEC_SEQ
N_TOK = NP_ROWS + NS_ROWS
N_COND = 8

LANES = 128
SUBLANES = 8
MIB = 1024 * 1024


def _cparams(n_axes, vmem_mib):
    return pltpu.CompilerParams(dimension_semantics=("arbitrary",) * n_axes,
                                vmem_limit_bytes=vmem_mib * MIB)


def _modrow(i, tm):
    n_ctx = NP_ROWS // tm
    per_seq = DEC_SEQ // tm
    return jnp.where(i < n_ctx, 0, 1 + (i - n_ctx) // per_seq)


def _silu(x):
    return x * jax.nn.sigmoid(x)


def _ada_kernel(c_ref, w_ref, b_ref, o_ref):
    s = _silu(c_ref[...]).astype(BF16)
    o_ref[...] = jnp.dot(s, w_ref[...].astype(BF16), preferred_element_type=F32) + b_ref[...]


def _ada(cond, w_ada, b_ada):
    tn = 1024
    n6 = 6 * D_MODEL
    return pl.pallas_call(
        _ada_kernel,
        grid=(DEPTH, n6 // tn),
        in_specs=[pl.BlockSpec((N_COND, D_MODEL), lambda l, j: (0, 0)),
                  pl.BlockSpec((None, D_MODEL, tn), lambda l, j: (l, 0, j)),
                  pl.BlockSpec((None, 1, tn), lambda l, j: (l, 0, j))],
        out_specs=pl.BlockSpec((None, N_COND, tn), lambda l, j: (l, 0, j)),
        out_shape=jax.ShapeDtypeStruct((DEPTH, N_COND, n6), F32),
        compiler_params=_cparams(2, 40),
        name="ada",
    )(cond, w_ada, b_ada.reshape(DEPTH, 1, n6))


def _modnorm_kernel(x_ref, g_ref, m_ref, o_ref, *, shift_row, scale_row):
    x = x_ref[...]
    y = x * lax.rsqrt(jnp.mean(x * x, -1, keepdims=True) + EPS) * g_ref[...]
    y = y * (1.0 + m_ref[scale_row:scale_row + 1, :]) + m_ref[shift_row:shift_row + 1, :]
    o_ref[...] = y.astype(o_ref.dtype)


def _modnorm(x, g, mod, shift_row, scale_row):
    tm = 256
    return pl.pallas_call(
        functools.partial(_modnorm_kernel, shift_row=shift_row, scale_row=scale_row),
        grid=(N_TOK // tm,),
        in_specs=[pl.BlockSpec((tm, D_MODEL), lambda i: (i, 0)),
                  pl.BlockSpec((1, D_MODEL), lambda i: (0, 0)),
                  pl.BlockSpec((None, 6, D_MODEL), lambda i: (_modrow(i, tm), 0, 0))],
        out_specs=pl.BlockSpec((tm, D_MODEL), lambda i: (i, 0)),
        out_shape=jax.ShapeDtypeStruct((N_TOK, D_MODEL), BF16),
        compiler_params=_cparams(1, 32),
        name="modnorm",
    )(x, g.reshape(1, D_MODEL), mod)


def _mm_kernel(a_ref, w_ref, o_ref):
    o_ref[...] = jnp.dot(a_ref[...], w_ref[...], preferred_element_type=F32).astype(o_ref.dtype)


def _matmul(a, w, out_dtype, tm, tn, vmem_mib, name):
    m, k = a.shape
    n = w.shape[1]
    return pl.pallas_call(
        _mm_kernel,
        grid=(m // tm, n // tn),
        in_specs=[pl.BlockSpec((tm, k), lambda i, j: (i, 0)),
                  pl.BlockSpec((k, tn), lambda i, j: (0, j))],
        out_specs=pl.BlockSpec((tm, tn), lambda i, j: (i, j)),
        out_shape=jax.ShapeDtypeStruct((m, n), out_dtype),
        compiler_params=_cparams(2, vmem_mib),
        name=name,
    )(a, w)


def _proj_kernel(*refs, n_full, n_split, gate_row, final, tm):
    full_refs = refs[:n_full]
    split_refs = refs[n_full:n_full + 2 * n_split]
    pos = n_full + 2 * n_split
    w_refs = refs[pos:pos + n_full + n_split]
    pos += n_full + n_split
    x_ref, mg_ref, g_ref = refs[pos:pos + 3]
    pos += 3
    if final:
        (y_ref,) = refs[pos:]
    else:
        mn_ref, xo_ref, h_ref = refs[pos:]
    i = pl.program_id(0)
    half = tm // 2

    def body(lhs_refs):
        for r in range(2):
            rows = slice(r * half, (r + 1) * half)
            acc = jnp.dot(lhs_refs[0][rows, :], w_refs[0][...], preferred_element_type=F32)
            for a_ref, w_ref in zip(lhs_refs[1:], w_refs[1:]):
                acc = acc + jnp.dot(a_ref[rows, :], w_ref[...], preferred_element_type=F32)
            xn = x_ref[rows, :] + mg_ref[gate_row:gate_row + 1, :] * acc
            y = xn * lax.rsqrt(jnp.mean(xn * xn, -1, keepdims=True) + EPS) * g_ref[...]
            if final:
                y_ref[rows, :] = y
            else:
                xo_ref[rows, :] = xn
                h_ref[rows, :] = (y * (1.0 + mn_ref[1:2, :]) + mn_ref[0:1, :]).astype(h_ref.dtype)

    if n_split:
        n_ctx = NP_ROWS // tm

        @pl.when(i < n_ctx)
        def _():
            body(list(full_refs) + list(split_refs[0::2]))

        @pl.when(i >= n_ctx)
        def _():
            body(list(full_refs) + list(split_refs[1::2]))
    else:
        body(list(full_refs))


def _proj_res_norm(full, split, ws, x, mod_gate, gate_row, norm_g, mod_next, tm, vmem_mib, name,
                   row0=0, n_rows=N_TOK):
    final = mod_next is None
    n_ctx = NP_ROWS // tm
    off = row0 // tm
    assert not (split and n_rows != N_TOK)
    once = pl.Buffered(1)
    in_specs = [pl.BlockSpec((tm, a.shape[1]), lambda i: (i + off, 0)) for a in full]
    args = list(full)
    for a_ctx, a_lat in split:
        in_specs.append(pl.BlockSpec((tm, a_ctx.shape[1]), lambda i: (jnp.minimum(i, n_ctx - 1), 0)))
        in_specs.append(pl.BlockSpec((tm, a_lat.shape[1]), lambda i: (jnp.maximum(i - n_ctx, 0), 0)))
        args += [a_ctx, a_lat]
    in_specs += [pl.BlockSpec(w.shape, lambda i: (0, 0), pipeline_mode=once) for w in ws]
    args += list(ws)
    row_spec = pl.BlockSpec((tm, D_MODEL), lambda i: (i, 0))
    in_specs += [pl.BlockSpec((tm, D_MODEL), lambda i: (i + off, 0)),
                 pl.BlockSpec((None, 6, D_MODEL), lambda i: (_modrow(i + off, tm), 0, 0)),
                 pl.BlockSpec((1, D_MODEL), lambda i: (0, 0))]
    args += [x, mod_gate, norm_g.reshape(1, D_MODEL)]
    if final:
        out_specs = row_spec
        out_shape = jax.ShapeDtypeStruct((n_rows, D_MODEL), F32)
    else:
        in_specs.append(pl.BlockSpec((None, 2, D_MODEL), lambda i: (_modrow(i + off, tm), 0, 0)))
        args.append(mod_next)
        out_specs = [row_spec, row_spec]
        out_shape = [jax.ShapeDtypeStruct((n_rows, D_MODEL), F32),
                     jax.ShapeDtypeStruct((n_rows, D_MODEL), BF16)]
    return pl.pallas_call(
        functools.partial(_proj_kernel, n_full=len(full), n_split=len(split), gate_row=gate_row,
                          final=final, tm=tm),
        grid=(n_rows // tm,),
        in_specs=in_specs,
        out_specs=out_specs,
        out_shape=out_shape,
        compiler_params=_cparams(1, vmem_mib),
        name=name,
    )(*args)


CONV_TM = 256
CONV_HALO = 16


def _convmod_kernel(a_ref, g_ref, ap_ref, gp_ref, an_ref, gn_ref, cw_ref, cb_ref, lng_ref, lnb_ref,
                    o_ref, ext_ref, y_ref, win_ref):
    tm = CONV_TM
    i = pl.program_id(0)
    n_ctx = NP_ROWS // tm
    per_seq = DEC_SEQ // tm
    j = (i - n_ctx) % per_seq
    latent = i >= n_ctx
    has_prev = jnp.logical_and(latent, j != 0)
    has_next = jnp.logical_and(latent, j != per_seq - 1)

    def glu(a, g):
        return a * jax.nn.sigmoid(g)

    ext_ref[CONV_HALO:CONV_HALO + tm, :] = glu(a_ref[...], g_ref[...])
    ext_ref[0:CONV_HALO, :] = jnp.where(has_prev, glu(ap_ref[...], gp_ref[...]), 0.0)
    ext_ref[CONV_HALO + tm:, :] = jnp.where(has_next, glu(an_ref[...], gn_ref[...]), 0.0)

    off = CONV_HALO - CONV_W // 2
    span = tm + SUBLANES * ((off + CONV_W - 1) // SUBLANES)
    hrows = tm // 2
    for cb in range(CONV_CH // LANES):
        cs = slice(cb * LANES, (cb + 1) * LANES)
        for r in range(SUBLANES):
            win_ref[r, 0:span, :] = ext_ref[r:r + span, cs]
        for base in range(0, tm, hrows):
            acc = jnp.zeros((hrows, LANES), F32) + cb_ref[:, cs]
            for k in range(CONV_W):
                r = (off + k) % SUBLANES
                m = (off + k) // SUBLANES * SUBLANES
                acc = acc + cw_ref[k:k + 1, cs] * win_ref[r, base + m:base + m + hrows, :]
            y_ref[base:base + hrows, cs] = acc

    y = y_ref[...]
    mu = jnp.mean(y, -1, keepdims=True)
    yc = y - mu
    z = yc * lax.rsqrt(jnp.mean(yc * yc, -1, keepdims=True) + EPS) * lng_ref[...] + lnb_ref[...]
    o_ref[...] = _silu(z).astype(o_ref.dtype)


def _convmod(p, cw, cb, ln_g, ln_b):
    tm = CONV_TM
    hb = tm // CONV_HALO
    last = N_TOK // CONV_HALO - 1
    prev_map = lambda i: (jnp.maximum(i * hb - 1, 0), 0)
    prev_map_g = lambda i: (jnp.maximum(i * hb - 1, 0), 1)
    next_map = lambda i: (jnp.minimum((i + 1) * hb, last), 0)
    next_map_g = lambda i: (jnp.minimum((i + 1) * hb, last), 1)
    vec = pl.BlockSpec((1, CONV_CH), lambda i: (0, 0))
    return pl.pallas_call(
        _convmod_kernel,
        grid=(N_TOK // tm,),
        in_specs=[pl.BlockSpec((tm, CONV_CH), lambda i: (i, 0)),
                  pl.BlockSpec((tm, CONV_CH), lambda i: (i, 1)),
                  pl.BlockSpec((CONV_HALO, CONV_CH), prev_map),
                  pl.BlockSpec((CONV_HALO, CONV_CH), prev_map_g),
                  pl.BlockSpec((CONV_HALO, CONV_CH), next_map),
                  pl.BlockSpec((CONV_HALO, CONV_CH), next_map_g),
                  pl.BlockSpec((CONV_W, CONV_CH), lambda i: (0, 0)),
                  vec, vec, vec],
        out_specs=pl.BlockSpec((tm, CONV_CH), lambda i: (i, 0)),
        out_shape=jax.ShapeDtypeStruct((N_TOK, CONV_CH), BF16),
        scratch_shapes=[pltpu.VMEM((tm + 2 * CONV_HALO, CONV_CH), F32),
                        pltpu.VMEM((tm, CONV_CH), F32),
                        pltpu.VMEM((SUBLANES, tm + 2 * CONV_HALO, LANES), F32)],
        compiler_params=_cparams(1, 32),
        name="convmod",
    )(p, p, p, p, p, p, cw, cb.reshape(1, -1), ln_g.reshape(1, -1), ln_b.reshape(1, -1))


ATT_GW = GQA_GROUP * HEAD_DIM
ATT_Q_BLK = 2 * CONV_CH // ATT_GW
ATT_K_BLK = (2 * CONV_CH + N_Q_HEADS * HEAD_DIM) // HEAD_DIM
ATT_V_BLK = ATT_K_BLK + N_KV_HEADS
ATT_TQ = 256


def _rms_rows(x, g):
    return x * lax.rsqrt(jnp.mean(x * x, -1, keepdims=True) + EPS) * g


def _rope(x, cos, sin):
    lane = lax.broadcasted_iota(jnp.int32, x.shape, 1)
    partner = jnp.where((lane & 1) == 0, pltpu.roll(x, HEAD_DIM - 1, 1), pltpu.roll(x, 1, 1))
    return x * cos + partner * sin


def _attend_group(q_ref, qg_ref, rotate, k, v, o_ref):
    for g in range(GQA_GROUP):
        hs = slice(g * HEAD_DIM, (g + 1) * HEAD_DIM)
        q = _rms_rows(q_ref[:, hs], qg_ref[...])
        if rotate is not None:
            q = rotate(q)
        q = (q * HEAD_DIM ** -0.5).astype(BF16)
        s = lax.dot_general(q, k, (((1,), (1,)), ((), ())), preferred_element_type=F32)
        p = jnp.exp(s - jnp.max(s, -1, keepdims=True))
        denom = jnp.sum(p, -1, keepdims=True)
        o = jnp.dot(p.astype(BF16), v, preferred_element_type=F32) / denom
        o_ref[:, hs] = o.astype(o_ref.dtype)


def _attn_ctx_kernel(q_ref, k_ref, v_ref, qg_ref, kg_ref, o_ref, kf_ref, vf_ref):
    kn = _rms_rows(k_ref[...], kg_ref[...])
    v = v_ref[...]
    kf_ref[...] = kn
    vf_ref[...] = v
    _attend_group(q_ref, qg_ref, None, kn.astype(BF16), v.astype(BF16), o_ref)


def _attn_ctx(p, qg, kg):
    nkv = N_KV_HEADS * HEAD_DIM
    vec = pl.BlockSpec((1, HEAD_DIM), lambda b, h: (0, 0))
    kv_out = pl.BlockSpec((SEQ, HEAD_DIM), lambda b, h: (b, h))
    return pl.pallas_call(
        _attn_ctx_kernel,
        grid=(BATCH, N_KV_HEADS),
        in_specs=[pl.BlockSpec((SEQ, ATT_GW), lambda b, h: (b, ATT_Q_BLK + h)),
                  pl.BlockSpec((SEQ, HEAD_DIM), lambda b, h: (b, ATT_K_BLK + h)),
                  pl.BlockSpec((SEQ, HEAD_DIM), lambda b, h: (b, ATT_V_BLK + h)),
                  vec, vec],
        out_specs=[pl.BlockSpec((SEQ, ATT_GW), lambda b, h: (b, h)), kv_out, kv_out],
        out_shape=[jax.ShapeDtypeStruct((NP_ROWS, N_Q_HEADS * HEAD_DIM), BF16),
                   jax.ShapeDtypeStruct((NP_ROWS, nkv), F32),
                   jax.ShapeDtypeStruct((NP_ROWS, nkv), F32)],
        compiler_params=_cparams(2, 32),
        name="attn_ctx",
    )(p, p, p, qg.reshape(1, -1), kg.reshape(1, -1))


def _attn_lat_kernel(q_ref, k_ref, v_ref, ck_ref, cv_ref, cosq_ref, sinq_ref, cosk_ref, sink_ref,
                     qg_ref, kg_ref, o_ref, ks_ref, vs_ref):
    @pl.when(pl.program_id(2) == 0)
    def _():
        for r0 in range(0, DEC_SEQ, ATT_TQ):
            rows = slice(r0, r0 + ATT_TQ)
            kn = _rms_rows(k_ref[rows, :], kg_ref[...])
            ks_ref[rows, :] = _rope(kn, cosk_ref[rows, :], sink_ref[rows, :]).astype(BF16)
            vs_ref[rows, :] = v_ref[rows, :].astype(BF16)
        ks_ref[DEC_SEQ:, :] = ck_ref[...].astype(BF16)
        vs_ref[DEC_SEQ:, :] = cv_ref[...].astype(BF16)

    cos = cosq_ref[...]
    sin = sinq_ref[...]
    _attend_group(q_ref, qg_ref, lambda x: _rope(x, cos, sin), ks_ref[...], vs_ref[...], o_ref)


def _attn_lat(p, cache_k, cache_v, cos_tab, sin_tab, qg, kg):
    tq = ATT_TQ
    n_q = DEC_SEQ // tq
    row0 = NP_ROWS // tq
    seq0 = NP_ROWS // DEC_SEQ
    vec = pl.BlockSpec((1, HEAD_DIM), lambda b, h, i: (0, 0))
    cache = pl.BlockSpec((None, PAST_LEN, HEAD_DIM), lambda b, h, i: (b, 0, h))
    tab_q = pl.BlockSpec((tq, HEAD_DIM), lambda b, h, i: (i, 0))
    tab_k = pl.BlockSpec((DEC_SEQ, HEAD_DIM), lambda b, h, i: (0, 0))
    return pl.pallas_call(
        _attn_lat_kernel,
        grid=(DEC_BATCH, N_KV_HEADS, n_q),
        in_specs=[pl.BlockSpec((tq, ATT_GW), lambda b, h, i: (row0 + b * n_q + i, ATT_Q_BLK + h)),
                  pl.BlockSpec((DEC_SEQ, HEAD_DIM), lambda b, h, i: (seq0 + b, ATT_K_BLK + h)),
                  pl.BlockSpec((DEC_SEQ, HEAD_DIM), lambda b, h, i: (seq0 + b, ATT_V_BLK + h)),
                  cache, cache, tab_q, tab_q, tab_k, tab_k, vec, vec],
        out_specs=pl.BlockSpec((tq, ATT_GW), lambda b, h, i: (b * n_q + i, h)),
        out_shape=jax.ShapeDtypeStruct((NS_ROWS, N_Q_HEADS * HEAD_DIM), BF16),
        scratch_shapes=[pltpu.VMEM((DEC_SEQ + PAST_LEN, HEAD_DIM), BF16),
                        pltpu.VMEM((DEC_SEQ + PAST_LEN, HEAD_DIM), BF16)],
        compiler_params=_cparams(3, 40),
        name="attn_lat",
    )(p, p, p, cache_k, cache_v, cos_tab, sin_tab, cos_tab, sin_tab, qg.reshape(1, -1), kg.reshape(1, -1))


def _rope_tables():
    rows = DEC_SEQ // GRID_W
    row = jnp.repeat(jnp.arange(rows), GRID_W).astype(F32)
    col = jnp.tile(jnp.arange(GRID_W), rows).astype(F32)
    n = HEAD_DIM // 4
    inv = ROPE_THETA ** (-jnp.arange(n, dtype=F32) / n)
    ang = jnp.concatenate([row[:, None] * inv, col[:, None] * inv], -1)
    cos = jnp.repeat(jnp.cos(ang), 2, axis=-1)
    sin = jnp.repeat(jnp.sin(ang), 2, axis=-1) * jnp.tile(jnp.array([-1.0, 1.0], F32), HEAD_DIM // 2)
    return cos, sin


GLA_CH = 128
GATE_TM = 512
GLA_TB = 256


def _gate_kernel(z_ref, wg_ref, bg_ref, o_ref):
    z = z_ref[...]
    r = lax.broadcasted_iota(jnp.int32, (GLA_CH, GLA_CH), 0)
    c = lax.broadcasted_iota(jnp.int32, (GLA_CH, GLA_CH), 1)
    for d in range(2):
        zz = jnp.dot(z, wg_ref[d], preferred_element_type=F32) + bg_ref[d]
        lg = (jnp.minimum(zz, 0.0) - jnp.log(1.0 + jnp.exp(-jnp.abs(zz)))) * (1.0 / GLA_TAU)
        hi = lg.astype(BF16)
        rest = lg - hi.astype(F32)
        mid = rest.astype(BF16)
        lo = (rest - mid.astype(F32)).astype(BF16)
        tri = jnp.where((r >= c) if d == 0 else (r <= c), 1.0, 0.0).astype(BF16)
        for r0 in range(0, GATE_TM, GLA_CH):
            rows = slice(r0, r0 + GLA_CH)
            o_ref[d, rows, :] = (jnp.dot(tri, hi[rows, :], preferred_element_type=F32)
                                 + jnp.dot(tri, mid[rows, :], preferred_element_type=F32)
                                 + jnp.dot(tri, lo[rows, :], preferred_element_type=F32))


def _gla_gates(z, wg_pad, bg):
    tm = GATE_TM
    return pl.pallas_call(
        _gate_kernel,
        grid=(N_TOK // tm,),
        in_specs=[pl.BlockSpec((tm, LANES), lambda i: (i, 0)),
                  pl.BlockSpec((2, LANES, GLA_HK), lambda i: (0, 0, 0)),
                  pl.BlockSpec((2, 1, GLA_HK), lambda i: (0, 0, 0))],
        out_specs=pl.BlockSpec((2, tm, GLA_HK), lambda i: (0, i, 0)),
        out_shape=jax.ShapeDtypeStruct((2, N_TOK, GLA_HK), F32),
        compiler_params=_cparams(1, 32),
        name="gla_gate",
    )(z, wg_pad, bg.reshape(2, 1, GLA_HK))


def _gla_kernel(*refs, n_t, t_len, has_h0, want_hfin):
    rest = list(refs)
    qf_ref, kf_ref, vf_ref, bf_ref = (rest.pop(0) for _ in range(4))
    if n_t == 1:
        qb_ref, kb_ref, vb_ref = qf_ref, kf_ref, vf_ref
    else:
        qb_ref, kb_ref, vb_ref = (rest.pop(0) for _ in range(3))
    bb_ref, r_ref, gn_ref = (rest.pop(0) for _ in range(3))
    h0f_ref = rest.pop(0) if has_h0 else None
    h0b_ref = rest.pop(0) if has_h0 else None
    y_ref = rest.pop(0)
    hff_ref = rest.pop(0) if want_hfin else None
    hfb_ref = rest.pop(0) if want_hfin else None
    htf_ref, htb_ref, o_ref = rest
    t = pl.program_id(2)
    L = GLA_CH
    mid = L // 2

    @pl.when(t == 0)
    def _():
        if has_h0:
            htf_ref[...] = h0f_ref[...].T
            htb_ref[...] = h0b_ref[...].T
        else:
            htf_ref[...] = jnp.zeros_like(htf_ref)
            htb_ref[...] = jnp.zeros_like(htb_ref)
        o_ref[...] = jnp.zeros_like(o_ref)

    row = lax.broadcasted_iota(jnp.int32, (L, L), 0)
    col = lax.broadcasted_iota(jnp.int32, (L, L), 1)
    nt = (((1,), (1,)), ((), ()))
    tn = (((0,), (0,)), ((), ()))

    def chunk(q_ref, k_ref, v_ref, b_ref, ht_ref, r0, out0, reverse):
        rows = slice(r0, r0 + L)
        q = q_ref[rows, :].astype(F32) * GLA_DK ** -0.5
        k = k_ref[rows, :].astype(F32)
        v = v_ref[rows, :]
        b = b_ref[rows, :]
        b_end = b[0:1, :] if reverse else b[L - 1:L, :]
        b_mid = b[mid:mid + 1, :] if reverse else b[mid - 1:mid, :]
        qe = (q * jnp.exp(b - b_mid)).astype(BF16)
        ke = (k * jnp.exp(b_mid - b)).astype(BF16)
        a = lax.dot_general(qe, ke, nt, preferred_element_type=F32)
        a = jnp.where((row <= col) if reverse else (row >= col), a, 0.0).astype(BF16)
        ht = ht_ref[...]
        o = lax.dot_general((q * jnp.exp(b)).astype(BF16), ht.astype(BF16), nt, preferred_element_type=F32)
        o = o + jnp.dot(a, v, preferred_element_type=F32)
        o_ref[pl.ds(out0, L), :] += o
        kd = (k * jnp.exp(b_end - b)).astype(BF16)
        ht_ref[...] = ht * jnp.exp(b_end) + lax.dot_general(v, kd, tn, preferred_element_type=F32)

    n_c = GLA_TB // L
    for c in range(n_c):
        chunk(qf_ref, kf_ref, vf_ref, bf_ref, htf_ref, c * L,
              pl.multiple_of(t * GLA_TB + c * L, L), False)
        cb = n_c - 1 - c
        chunk(qb_ref, kb_ref, vb_ref, bb_ref, htb_ref, cb * L,
              pl.multiple_of((n_t - 1 - t) * GLA_TB + cb * L, L), True)

    @pl.when(t == n_t - 1)
    def _():
        for r0 in range(0, t_len, GLA_TB):
            rows = slice(r0, r0 + GLA_TB)
            o = o_ref[rows, :]
            y = o * lax.rsqrt(jnp.mean(o * o, -1, keepdims=True) + EPS) * gn_ref[...]
            y_ref[rows, :] = (y * _silu(r_ref[rows, :].astype(F32))).astype(y_ref.dtype)
        if want_hfin:
            hff_ref[...] = htf_ref[...].T
            hfb_ref[...] = htb_ref[...].T


def _gla(p, bcum, norm_g, h0f, h0b, row0, n_b, t_len, want_hfin, name):
    n_t = t_len // GLA_TB
    blk0 = row0 // GLA_TB
    seq0 = row0 // t_len
    has_h0 = h0f is not None
    k_blk = GLA_HK // GLA_DK
    v_blk = 2 * GLA_HK // GLA_DV
    r_blk = (2 * GLA_HK + GLA_HV) // GLA_DV

    def fwd(b_, t):
        return blk0 + b_ * n_t + t

    def bwd(b_, t):
        return blk0 + b_ * n_t + (n_t - 1 - t)

    def qkv_specs(rb):
        return [pl.BlockSpec((GLA_TB, GLA_DK), lambda b_, h, t: (rb(b_, t), h)),
                pl.BlockSpec((GLA_TB, GLA_DK), lambda b_, h, t: (rb(b_, t), k_blk + h)),
                pl.BlockSpec((GLA_TB, GLA_DV), lambda b_, h, t: (rb(b_, t), v_blk + h))]

    def decay_spec(rb):
        return pl.BlockSpec((GLA_TB, GLA_DK), lambda b_, h, t: (rb(b_, t), h))

    state = pl.BlockSpec((None, None, GLA_DK, GLA_DV), lambda b_, h, t: (b_, h, 0, 0))
    in_specs = qkv_specs(fwd) + [decay_spec(fwd)]
    args = [p, p, p, bcum[0]]
    if n_t > 1:
        in_specs += qkv_specs(bwd)
        args += [p, p, p]
    in_specs += [decay_spec(bwd),
                 pl.BlockSpec((t_len, GLA_DV), lambda b_, h, t: (seq0 + b_, r_blk + h)),
                 pl.BlockSpec((1, GLA_DV), lambda b_, h, t: (0, 0))]
    args += [bcum[1], p, norm_g.reshape(1, GLA_DV)]
    if has_h0:
        in_specs += [state, state]
        args += [h0f, h0b]
    out_specs = [pl.BlockSpec((t_len, GLA_DV), lambda b_, h, t: (b_, h))]
    out_shape = [jax.ShapeDtypeStruct((n_b * t_len, GLA_HV), BF16)]
    if want_hfin:
        out_specs += [state, state]
        out_shape += [jax.ShapeDtypeStruct((n_b, GLA_HEADS, GLA_DK, GLA_DV), F32)] * 2
    return pl.pallas_call(
        functools.partial(_gla_kernel, n_t=n_t, t_len=t_len, has_h0=has_h0, want_hfin=want_hfin),
        grid=(n_b, GLA_HEADS, n_t),
        in_specs=in_specs,
        out_specs=out_specs,
        out_shape=out_shape,
        scratch_shapes=[pltpu.VMEM((GLA_DV, GLA_DK), F32), pltpu.VMEM((GLA_DV, GLA_DK), F32),
                        pltpu.VMEM((t_len, GLA_DV), F32)],
        compiler_params=_cparams(3, 40),
        name=name,
    )(*args)


FFN_TM = 2048
FFN_TN = 256


def _ffn_up_kernel(h_ref, wg_ref, wv_ref, cwg_ref, cwv_ref, cbg_ref, cbv_ref, o_ref):
    tm = FFN_TM
    i = pl.program_id(0)
    seq = jnp.where(i < NP_ROWS // tm, SEQ, DEC_SEQ)
    sub = lax.broadcasted_iota(jnp.int32, (SUBLANES, FFN_TN), 0)
    h = h_ref[...]

    def zero_at_edges(x, first):
        pieces = []
        for t in range(0, tm, SEQ):
            seg = x[t:t + SEQ]
            if first:
                at_edge = (t & (seq - 1)) == 0
                pieces += [jnp.where(at_edge & (sub == 0), 0.0, seg[:SUBLANES]), seg[SUBLANES:]]
            else:
                at_edge = ((t + SEQ) & (seq - 1)) == 0
                pieces += [seg[:SEQ - SUBLANES],
                           jnp.where(at_edge & (sub == SUBLANES - 1), 0.0, seg[SEQ - SUBLANES:])]
        return jnp.concatenate(pieces, 0)

    def conv(u, cw_ref, cb_ref):
        up = zero_at_edges(pltpu.roll(u, 1, 0), True)
        un = zero_at_edges(pltpu.roll(u, tm - 1, 0), False)
        return cw_ref[0:1, :] * up + cw_ref[1:2, :] * u + cw_ref[2:3, :] * un + cb_ref[...]

    g = conv(jnp.dot(h, wg_ref[...], preferred_element_type=F32), cwg_ref, cbg_ref)
    val = conv(jnp.dot(h, wv_ref[...], preferred_element_type=F32), cwv_ref, cbv_ref)
    o_ref[...] = (_silu(g) * val).astype(o_ref.dtype)


def _ffn_up(h, w_up, cw, cb):
    tm, tn = FFN_TM, FFN_TN
    nj = D_FF // tn
    return pl.pallas_call(
        _ffn_up_kernel,
        grid=(N_TOK // tm, nj),
        in_specs=[pl.BlockSpec((tm, D_MODEL), lambda i, j: (i, 0)),
                  pl.BlockSpec((D_MODEL, tn), lambda i, j: (0, j)),
                  pl.BlockSpec((D_MODEL, tn), lambda i, j: (0, nj + j)),
                  pl.BlockSpec((3, tn), lambda i, j: (0, j)),
                  pl.BlockSpec((3, tn), lambda i, j: (0, nj + j)),
                  pl.BlockSpec((1, tn), lambda i, j: (0, j)),
                  pl.BlockSpec((1, tn), lambda i, j: (0, nj + j))],
        out_specs=pl.BlockSpec((tm, tn), lambda i, j: (i, j)),
        out_shape=jax.ShapeDtypeStruct((N_TOK, D_FF), BF16),
        compiler_params=_cparams(2, 48),
        name="ffn_up",
    )(h, w_up, w_up, cw, cw, cb.reshape(1, -1), cb.reshape(1, -1))


def kernel(x_prompt, x_sample, c, cache_k, cache_v, state_gla_fwd, state_gla_bwd, c_ctx, norm1_g, norm2_g,
           w_ada, b_ada, even_w_in, conv_w, conv_b, conv_ln_g, conv_ln_b, q_norm_g, k_norm_g, even_w_out,
           odd_w_in, gla_w_gate, gla_b_gate, gla_norm_g, odd_w_out, ffn_w_up, ffn_conv_w, ffn_conv_b,
           ffn_w_down, final_norm_g):
    x = jnp.concatenate([x_prompt.reshape(NP_ROWS, D_MODEL), x_sample.reshape(NS_ROWS, D_MODEL)], 0)
    cond = jnp.concatenate([c_ctx[None, :], c, jnp.zeros((N_COND - 1 - DEC_BATCH, D_MODEL), F32)], 0)
    mod_all = _ada(cond, w_ada, b_ada).reshape(DEPTH, N_COND, 6, D_MODEL)
    cos_tab, sin_tab = _rope_tables()
    nkv = N_KV_HEADS * HEAD_DIM

    new_k, new_v, new_sf, new_sb = [], [], [], []
    h = _modnorm(x, norm1_g[0], mod_all[0], 0, 1)
    for l in range(DEPTH):
        mod = mod_all[l]
        mod_norm2 = mod[:, 3:5]
        if l % 2 == 0:
            e = l // 2
            p = _matmul(h, even_w_in[e].astype(BF16), F32, 1024, EVEN_IN // 2, 48, "even_in")
            a = _convmod(p, conv_w[e], conv_b[e], conv_ln_g[e], conv_ln_b[e])
            o_p, k_new, v_new = _attn_ctx(p, q_norm_g[e], k_norm_g[e])
            new_k.append(k_new.reshape(BATCH, SEQ, N_KV_HEADS, HEAD_DIM))
            new_v.append(v_new.reshape(BATCH, SEQ, N_KV_HEADS, HEAD_DIM))
            o_s = _attn_lat(p, cache_k[:, e].reshape(DEC_BATCH, PAST_LEN, nkv),
                            cache_v[:, e].reshape(DEC_BATCH, PAST_LEN, nkv), cos_tab, sin_tab,
                            q_norm_g[e], k_norm_g[e])
            w_out = even_w_out[e].astype(BF16)
            x, h2 = _proj_res_norm([a], [(o_p, o_s)], [w_out[:CONV_CH], w_out[CONV_CH:]], x, mod, 2,
                                   norm2_g[l], mod_norm2, 512, 48, "even_out")
        else:
            od = l // 2
            w_in = odd_w_in[od]
            p = _matmul(h, w_in[:, :GLA_MAIN].astype(BF16), BF16, 1024, 1024, 40, "odd_in")
            w_z = jnp.pad(w_in[:, GLA_MAIN:], ((0, 0), (0, LANES - 2 * GLA_RANK))).astype(BF16)
            z = _matmul(h, w_z, BF16, 1024, LANES, 32, "odd_z")
            wg = gla_w_gate[od]
            wg_pad = jnp.zeros((2, LANES, GLA_HK), F32)
            wg_pad = wg_pad.at[0, :GLA_RANK].set(wg[0]).at[1, GLA_RANK:2 * GLA_RANK].set(wg[1]).astype(BF16)
            bcum = _gla_gates(z, wg_pad, gla_b_gate[od])
            y_p, hf, hb = _gla(p, bcum, gla_norm_g[od], None, None, 0, BATCH, SEQ, True, "gla_ctx")
            new_sf.append(hf)
            new_sb.append(hb)
            (y_s,) = _gla(p, bcum, gla_norm_g[od], state_gla_fwd[:, od], state_gla_bwd[:, od], NP_ROWS,
                          DEC_BATCH, DEC_SEQ, False, "gla_lat")
            x, h2 = _proj_res_norm([], [(y_p, y_s)], [odd_w_out[od].astype(BF16)], x, mod, 2, norm2_g[l],
                                   mod_norm2, 512, 48, "odd_out")
        u = _ffn_up(h2, ffn_w_up[l].astype(BF16), ffn_conv_w[l], ffn_conv_b[l])
        w_down = ffn_w_down[l].astype(BF16)
        if l + 1 < DEPTH:
            x, h = _proj_res_norm([u], [], [w_down], x, mod, 5, norm1_g[l + 1], mod_all[l + 1][:, 0:2],
                                  256, 56, "ffn_down")
        else:
            y_p = _proj_res_norm([u], [], [w_down], x, mod, 5, final_norm_g, None, 256, 56, "ffn_down_ctx",
                                 0, NP_ROWS)
            y_s = _proj_res_norm([u], [], [w_down], x, mod, 5, final_norm_g, None, 256, 56, "ffn_down_lat",
                                 NP_ROWS, NS_ROWS)

    y_prompt = y_p.reshape(BATCH, SEQ, D_MODEL)
    y_sample = y_s.reshape(DEC_BATCH, DEC_SEQ, D_MODEL)
    return (y_prompt, y_sample, jnp.stack(new_k, 1), jnp.stack(new_v, 1), jnp.stack(new_sf, 1),
            jnp.stack(new_sb, 1))
```

```python
import functools

import jax
import jax.numpy as jnp
from jax import lax
from jax.experimental import pallas as pl
from jax.experimental.pallas import tpu as pltpu

F32 = jnp.float32
BF16 = jnp.bfloat16

D_MODEL = 2048
BATCH = 32
SEQ = 256
DEPTH = 4
DEC_BATCH = 4
DEC_SEQ = 2048
PAST_LEN = 512
GRID_W = 64
CONV_CH = D_MODEL // 2
CONV_W = 31
HEAD_DIM = 128
N_Q_HEADS = 8
N_KV_HEADS = 2
GQA_GROUP = N_Q_HEADS // N_KV_HEADS
ROPE_THETA = 10000.0
EVEN_IN = 2 * CONV_CH + (N_Q_HEADS + 2 * N_KV_HEADS) * HEAD_DIM
GLA_HEADS = 4
GLA_DK = 256
GLA_DV = 512
GLA_RANK = 16
GLA_TAU = 16.0
GLA_HK = GLA_HEADS * GLA_DK
GLA_HV = GLA_HEADS * GLA_DV
GLA_MAIN = 2 * GLA_HK + 2 * GLA_HV
D_FF = 11 * D_MODEL // 4
EPS = 1e-6

NP_ROWS = BATCH * SEQ
NS_ROWS = DEC_BATCH * DEC_SEQ
N_TOK = NP_ROWS + NS_ROWS
N_COND = 8

LANES = 128
SUBLANES = 8
MIB = 1024 * 1024


def _cparams(n_axes, vmem_mib):
    return pltpu.CompilerParams(dimension_semantics=("arbitrary",) * n_axes,
                                vmem_limit_bytes=vmem_mib * MIB)


def _modrow(i, tm):
    n_ctx = NP_ROWS // tm
    per_seq = DEC_SEQ // tm
    return jnp.where(i < n_ctx, 0, 1 + (i - n_ctx) // per_seq)


def _silu(x):
    return x * jax.nn.sigmoid(x)


def _ada_kernel(c_ref, w_ref, b_ref, o_ref):
    s = _silu(c_ref[...]).astype(BF16)
    o_ref[...] = jnp.dot(s, w_ref[...].astype(BF16), preferred_element_type=F32) + b_ref[...]


def _ada(cond, w_ada, b_ada):
    tn = 1024
    n6 = 6 * D_MODEL
    return pl.pallas_call(
        _ada_kernel,
        grid=(DEPTH, n6 // tn),
        in_specs=[pl.BlockSpec((N_COND, D_MODEL), lambda l, j: (0, 0)),
                  pl.BlockSpec((None, D_MODEL, tn), lambda l, j: (l, 0, j)),
                  pl.BlockSpec((None, 1, tn), lambda l, j: (l, 0, j))],
        out_specs=pl.BlockSpec((None, N_COND, tn), lambda l, j: (l, 0, j)),
        out_shape=jax.ShapeDtypeStruct((DEPTH, N_COND, n6), F32),
        compiler_params=_cparams(2, 40),
        name="ada",
    )(cond, w_ada, b_ada.reshape(DEPTH, 1, n6))


MODNORM_TM = 256


def _modnorm_kernel(xc_ref, xl_ref, g_ref, m_ref, o_ref, *, shift_row, scale_row):
    def body(x_ref):
        x = x_ref[...]
        y = x * lax.rsqrt(jnp.mean(x * x, -1, keepdims=True) + EPS) * g_ref[...]
        y = y * (1.0 + m_ref[scale_row:scale_row + 1, :]) + m_ref[shift_row:shift_row + 1, :]
        o_ref[...] = y.astype(o_ref.dtype)

    i = pl.program_id(0)
    n_ctx = NP_ROWS // MODNORM_TM

    @pl.when(i < n_ctx)
    def _():
        body(xc_ref)

    @pl.when(i >= n_ctx)
    def _():
        body(xl_ref)


def _modnorm(x_ctx, x_lat, g, mod, shift_row, scale_row):
    tm = MODNORM_TM
    n_ctx = NP_ROWS // tm
    return pl.pallas_call(
        functools.partial(_modnorm_kernel, shift_row=shift_row, scale_row=scale_row),
        grid=(N_TOK // tm,),
        in_specs=[pl.BlockSpec((tm, D_MODEL), lambda i: (jnp.minimum(i, n_ctx - 1), 0)),
                  pl.BlockSpec((tm, D_MODEL), lambda i: (jnp.maximum(i - n_ctx, 0), 0)),
                  pl.BlockSpec((1, D_MODEL), lambda i: (0, 0)),
                  pl.BlockSpec((None, 6, D_MODEL), lambda i: (_modrow(i, tm), 0, 0))],
        out_specs=pl.BlockSpec((tm, D_MODEL), lambda i: (i, 0)),
        out_shape=jax.ShapeDtypeStruct((N_TOK, D_MODEL), BF16),
        compiler_params=_cparams(1, 32),
        name="modnorm",
    )(x_ctx, x_lat, g.reshape(1, D_MODEL), mod)


def _mm_kernel(a_ref, w_ref, o_ref):
    o_ref[...] = jnp.dot(a_ref[...], w_ref[...], preferred_element_type=F32).astype(o_ref.dtype)


def _matmul(a, w, out_dtype, tm, tn, vmem_mib, name):
    m, k = a.shape
    n = w.shape[1]
    return pl.pallas_call(
        _mm_kernel,
        grid=(m // tm, n // tn),
        in_specs=[pl.BlockSpec((tm, k), lambda i, j: (i, 0)),
                  pl.BlockSpec((k, tn), lambda i, j: (0, j))],
        out_specs=pl.BlockSpec((tm, tn), lambda i, j: (i, j)),
        out_shape=jax.ShapeDtypeStruct((m, n), out_dtype),
        compiler_params=_cparams(2, vmem_mib),
        name=name,
    )(a, w)


def _proj_kernel(*refs, n_full, n_split, x_split, gate_row, final, tm):
    full_refs = refs[:n_full]
    split_refs = refs[n_full:n_full + 2 * n_split]
    pos = n_full + 2 * n_split
    w_refs = refs[pos:pos + n_full + n_split]
    pos += n_full + n_split
    x_refs = refs[pos:pos + 1 + x_split]
    pos += 1 + x_split
    mg_ref, g_ref = refs[pos:pos + 2]
    pos += 2
    if final:
        (y_ref,) = refs[pos:]
    else:
        mn_ref, xo_ref, h_ref = refs[pos:]
    i = pl.program_id(0)
    half = tm // 2

    def body(lhs_refs, x_ref):
        for r in range(2):
            rows = slice(r * half, (r + 1) * half)
            acc = jnp.dot(lhs_refs[0][rows, :], w_refs[0][...], preferred_element_type=F32)
            for a_ref, w_ref in zip(lhs_refs[1:], w_refs[1:]):
                acc = acc + jnp.dot(a_ref[rows, :], w_ref[...], preferred_element_type=F32)
            xn = x_ref[rows, :] + mg_ref[gate_row:gate_row + 1, :] * acc
            y = xn * lax.rsqrt(jnp.mean(xn * xn, -1, keepdims=True) + EPS) * g_ref[...]
            if final:
                y_ref[rows, :] = y
            else:
                xo_ref[rows, :] = xn
                h_ref[rows, :] = (y * (1.0 + mn_ref[1:2, :]) + mn_ref[0:1, :]).astype(h_ref.dtype)

    if n_split or x_split:
        n_ctx = NP_ROWS // tm

        @pl.when(i < n_ctx)
        def _():
            body(list(full_refs) + list(split_refs[0::2]), x_refs[0])

        @pl.when(i >= n_ctx)
        def _():
            body(list(full_refs) + list(split_refs[1::2]), x_refs[-1])
    else:
        body(list(full_refs), x_refs[0])


def _proj_res_norm(full, split, ws, x, mod_gate, gate_row, norm_g, mod_next, tm, vmem_mib, name,
                   row0=0, n_rows=N_TOK):
    final = mod_next is None
    n_ctx = NP_ROWS // tm
    off = row0 // tm
    x_split = isinstance(x, tuple)
    assert not ((split or x_split) and n_rows != N_TOK)
    ctx_map = lambda i: (jnp.minimum(i, n_ctx - 1), 0)
    lat_map = lambda i: (jnp.maximum(i - n_ctx, 0), 0)
    once = pl.Buffered(1)
    in_specs = [pl.BlockSpec((tm, a.shape[1]), lambda i: (i + off, 0)) for a in full]
    args = list(full)
    for a_ctx, a_lat in split:
        in_specs += [pl.BlockSpec((tm, a_ctx.shape[1]), ctx_map), pl.BlockSpec((tm, a_lat.shape[1]), lat_map)]
        args += [a_ctx, a_lat]
    in_specs += [pl.BlockSpec(w.shape, lambda i: (0, 0), pipeline_mode=once) for w in ws]
    args += list(ws)
    row_spec = pl.BlockSpec((tm, D_MODEL), lambda i: (i, 0))
    if x_split:
        in_specs += [pl.BlockSpec((tm, D_MODEL), ctx_map), pl.BlockSpec((tm, D_MODEL), lat_map)]
        args += list(x)
    else:
        in_specs.append(pl.BlockSpec((tm, D_MODEL), lambda i: (i + off, 0)))
        args.append(x)
    in_specs += [pl.BlockSpec((None, 6, D_MODEL), lambda i: (_modrow(i + off, tm), 0, 0)),
                 pl.BlockSpec((1, D_MODEL), lambda i: (0, 0))]
    args += [mod_gate, norm_g.reshape(1, D_MODEL)]
    if final:
        out_specs = row_spec
        out_shape = jax.ShapeDtypeStruct((n_rows, D_MODEL), F32)
    else:
        in_specs.append(pl.BlockSpec((None, 2, D_MODEL), lambda i: (_modrow(i + off, tm), 0, 0)))
        args.append(mod_next)
        out_specs = [row_spec, row_spec]
        out_shape = [jax.ShapeDtypeStruct((n_rows, D_MODEL), F32),
                     jax.ShapeDtypeStruct((n_rows, D_MODEL), BF16)]
    return pl.pallas_call(
        functools.partial(_proj_kernel, n_full=len(full), n_split=len(split), x_split=int(x_split),
                          gate_row=gate_row, final=final, tm=tm),
        grid=(n_rows // tm,),
        in_specs=in_specs,
        out_specs=out_specs,
        out_shape=out_shape,
        compiler_params=_cparams(1, vmem_mib),
        name=name,
    )(*args)


CONV_TM = 256
CONV_HALO = 16


def _convmod_kernel(a_ref, g_ref, ap_ref, gp_ref, an_ref, gn_ref, cw_ref, cb_ref, lng_ref, lnb_ref,
                    o_ref, ext_ref, y_ref, win_ref):
    tm = CONV_TM
    i = pl.program_id(0)
    n_ctx = NP_ROWS // tm
    per_seq = DEC_SEQ // tm
    j = (i - n_ctx) % per_seq
    latent = i >= n_ctx
    has_prev = jnp.logical_and(latent, j != 0)
    has_next = jnp.logical_and(latent, j != per_seq - 1)

    def glu(a, g):
        return a * jax.nn.sigmoid(g)

    ext_ref[CONV_HALO:CONV_HALO + tm, :] = glu(a_ref[...], g_ref[...])
    ext_ref[0:CONV_HALO, :] = jnp.where(has_prev, glu(ap_ref[...], gp_ref[...]), 0.0)
    ext_ref[CONV_HALO + tm:, :] = jnp.where(has_next, glu(an_ref[...], gn_ref[...]), 0.0)

    off = CONV_HALO - CONV_W // 2
    span = tm + SUBLANES * ((off + CONV_W - 1) // SUBLANES)
    hrows = tm // 2
    for cb in range(CONV_CH // LANES):
        cs = slice(cb * LANES, (cb + 1) * LANES)
        for r in range(SUBLANES):
            win_ref[r, 0:span, :] = ext_ref[r:r + span, cs]
        for base in range(0, tm, hrows):
            acc = jnp.zeros((hrows, LANES), F32) + cb_ref[:, cs]
            for k in range(CONV_W):
                r = (off + k) % SUBLANES
                m = (off + k) // SUBLANES * SUBLANES
                acc = acc + cw_ref[k:k + 1, cs] * win_ref[r, base + m:base + m + hrows, :]
            y_ref[base:base + hrows, cs] = acc

    y = y_ref[...]
    mu = jnp.mean(y, -1, keepdims=True)
    yc = y - mu
    z = yc * lax.rsqrt(jnp.mean(yc * yc, -1, keepdims=True) + EPS) * lng_ref[...] + lnb_ref[...]
    o_ref[...] = _silu(z).astype(o_ref.dtype)


def _convmod(p, cw, cb, ln_g, ln_b):
    tm = CONV_TM
    hb = tm // CONV_HALO
    last = N_TOK // CONV_HALO - 1
    prev_map = lambda i: (jnp.maximum(i * hb - 1, 0), 0)
    prev_map_g = lambda i: (jnp.maximum(i * hb - 1, 0), 1)
    next_map = lambda i: (jnp.minimum((i + 1) * hb, last), 0)
    next_map_g = lambda i: (jnp.minimum((i + 1) * hb, last), 1)
    vec = pl.BlockSpec((1, CONV_CH), lambda i: (0, 0))
    return pl.pallas_call(
        _convmod_kernel,
        grid=(N_TOK // tm,),
        in_specs=[pl.BlockSpec((tm, CONV_CH), lambda i: (i, 0)),
                  pl.BlockSpec((tm, CONV_CH), lambda i: (i, 1)),
                  pl.BlockSpec((CONV_HALO, CONV_CH), prev_map),
                  pl.BlockSpec((CONV_HALO, CONV_CH), prev_map_g),
                  pl.BlockSpec((CONV_HALO, CONV_CH), next_map),
                  pl.BlockSpec((CONV_HALO, CONV_CH), next_map_g),
                  pl.BlockSpec((CONV_W, CONV_CH), lambda i: (0, 0)),
                  vec, vec, vec],
        out_specs=pl.BlockSpec((tm, CONV_CH), lambda i: (i, 0)),
        out_shape=jax.ShapeDtypeStruct((N_TOK, CONV_CH), BF16),
        scratch_shapes=[pltpu.VMEM((tm + 2 * CONV_HALO, CONV_CH), F32),
                        pltpu.VMEM((tm, CONV_CH), F32),
                        pltpu.VMEM((SUBLANES, tm + 2 * CONV_HALO, LANES), F32)],
        compiler_params=_cparams(1, 32),
        name="convmod",
    )(p, p, p, p, p, p, cw, cb.reshape(1, -1), ln_g.reshape(1, -1), ln_b.reshape(1, -1))


ATT_GW = GQA_GROUP * HEAD_DIM
ATT_Q_BLK = 2 * CONV_CH // ATT_GW
ATT_K_BLK = (2 * CONV_CH + N_Q_HEADS * HEAD_DIM) // HEAD_DIM
ATT_V_BLK = ATT_K_BLK + N_KV_HEADS
ATT_TQ = 256


def _rms_rows(x, g):
    return x * lax.rsqrt(jnp.mean(x * x, -1, keepdims=True) + EPS) * g


def _rope(x, cos, sin):
    lane = lax.broadcasted_iota(jnp.int32, x.shape, 1)
    partner = jnp.where((lane & 1) == 0, pltpu.roll(x, HEAD_DIM - 1, 1), pltpu.roll(x, 1, 1))
    return x * cos + partner * sin


def _attend_group(q_ref, qg_ref, rotate, k, v, o_ref):
    for g in range(GQA_GROUP):
        hs = slice(g * HEAD_DIM, (g + 1) * HEAD_DIM)
        q = _rms_rows(q_ref[:, hs], qg_ref[...])
        if rotate is not None:
            q = rotate(q)
        q = (q * HEAD_DIM ** -0.5).astype(BF16)
        s = lax.dot_general(q, k, (((1,), (1,)), ((), ())), preferred_element_type=F32)
        p = jnp.exp(s - jnp.max(s, -1, keepdims=True))
        denom = jnp.sum(p, -1, keepdims=True)
        o = jnp.dot(p.astype(BF16), v, preferred_element_type=F32) / denom
        o_ref[:, hs] = o.astype(o_ref.dtype)


def _attn_ctx_kernel(q_ref, k_ref, v_ref, qg_ref, kg_ref, o_ref, kf_ref, vf_ref):
    kn = _rms_rows(k_ref[...], kg_ref[...])
    v = v_ref[...]
    kf_ref[...] = kn
    vf_ref[...] = v
    _attend_group(q_ref, qg_ref, None, kn.astype(BF16), v.astype(BF16), o_ref)


def _attn_ctx(p, qg, kg):
    nkv = N_KV_HEADS * HEAD_DIM
    vec = pl.BlockSpec((1, HEAD_DIM), lambda b, h: (0, 0))
    kv_out = pl.BlockSpec((SEQ, HEAD_DIM), lambda b, h: (b, h))
    return pl.pallas_call(
        _attn_ctx_kernel,
        grid=(BATCH, N_KV_HEADS),
        in_specs=[pl.BlockSpec((SEQ, ATT_GW), lambda b, h: (b, ATT_Q_BLK + h)),
                  pl.BlockSpec((SEQ, HEAD_DIM), lambda b, h: (b, ATT_K_BLK + h)),
                  pl.BlockSpec((SEQ, HEAD_DIM), lambda b, h: (b, ATT_V_BLK + h)),
                  vec, vec],
        out_specs=[pl.BlockSpec((SEQ, ATT_GW), lambda b, h: (b, h)), kv_out, kv_out],
        out_shape=[jax.ShapeDtypeStruct((NP_ROWS, N_Q_HEADS * HEAD_DIM), BF16),
                   jax.ShapeDtypeStruct((NP_ROWS, nkv), F32),
                   jax.ShapeDtypeStruct((NP_ROWS, nkv), F32)],
        compiler_params=_cparams(2, 32),
        name="attn_ctx",
    )(p, p, p, qg.reshape(1, -1), kg.reshape(1, -1))


def _attn_lat_kernel(q_ref, k_ref, v_ref, ck_ref, cv_ref, cosq_ref, sinq_ref, cosk_ref, sink_ref,
                     qg_ref, kg_ref, o_ref, ks_ref, vs_ref):
    @pl.when(pl.program_id(2) == 0)
    def _():
        for r0 in range(0, DEC_SEQ, ATT_TQ):
            rows = slice(r0, r0 + ATT_TQ)
            kn = _rms_rows(k_ref[rows, :], kg_ref[...])
            ks_ref[rows, :] = _rope(kn, cosk_ref[rows, :], sink_ref[rows, :]).astype(BF16)
            vs_ref[rows, :] = v_ref[rows, :].astype(BF16)
        ks_ref[DEC_SEQ:, :] = ck_ref[...].astype(BF16)
        vs_ref[DEC_SEQ:, :] = cv_ref[...].astype(BF16)

    cos = cosq_ref[...]
    sin = sinq_ref[...]
    _attend_group(q_ref, qg_ref, lambda x: _rope(x, cos, sin), ks_ref[...], vs_ref[...], o_ref)


def _attn_lat(p, cache_k, cache_v, cos_tab, sin_tab, qg, kg):
    tq = ATT_TQ
    n_q = DEC_SEQ // tq
    row0 = NP_ROWS // tq
    seq0 = NP_ROWS // DEC_SEQ
    vec = pl.BlockSpec((1, HEAD_DIM), lambda b, h, i: (0, 0))
    cache = pl.BlockSpec((None, PAST_LEN, HEAD_DIM), lambda b, h, i: (b, 0, h))
    tab_q = pl.BlockSpec((tq, HEAD_DIM), lambda b, h, i: (i, 0))
    tab_k = pl.BlockSpec((DEC_SEQ, HEAD_DIM), lambda b, h, i: (0, 0))
    return pl.pallas_call(
        _attn_lat_kernel,
        grid=(DEC_BATCH, N_KV_HEADS, n_q),
        in_specs=[pl.BlockSpec((tq, ATT_GW), lambda b, h, i: (row0 + b * n_q + i, ATT_Q_BLK + h)),
                  pl.BlockSpec((DEC_SEQ, HEAD_DIM), lambda b, h, i: (seq0 + b, ATT_K_BLK + h)),
                  pl.BlockSpec((DEC_SEQ, HEAD_DIM), lambda b, h, i: (seq0 + b, ATT_V_BLK + h)),
                  cache, cache, tab_q, tab_q, tab_k, tab_k, vec, vec],
        out_specs=pl.BlockSpec((tq, ATT_GW), lambda b, h, i: (b * n_q + i, h)),
        out_shape=jax.ShapeDtypeStruct((NS_ROWS, N_Q_HEADS * HEAD_DIM), BF16),
        scratch_shapes=[pltpu.VMEM((DEC_SEQ + PAST_LEN, HEAD_DIM), BF16),
                        pltpu.VMEM((DEC_SEQ + PAST_LEN, HEAD_DIM), BF16)],
        compiler_params=_cparams(3, 40),
        name="attn_lat",
    )(p, p, p, cache_k, cache_v, cos_tab, sin_tab, cos_tab, sin_tab, qg.reshape(1, -1), kg.reshape(1, -1))


def _rope_tables():
    rows = DEC_SEQ // GRID_W
    row = jnp.repeat(jnp.arange(rows), GRID_W).astype(F32)
    col = jnp.tile(jnp.arange(GRID_W), rows).astype(F32)
    n = HEAD_DIM // 4
    inv = ROPE_THETA ** (-jnp.arange(n, dtype=F32) / n)
    ang = jnp.concatenate([row[:, None] * inv, col[:, None] * inv], -1)
    cos = jnp.repeat(jnp.cos(ang), 2, axis=-1)
    sin = jnp.repeat(jnp.sin(ang), 2, axis=-1) * jnp.tile(jnp.array([-1.0, 1.0], F32), HEAD_DIM // 2)
    return cos, sin


GLA_CH = 128
GATE_TM = 512
GLA_TB = 256


def _gate_kernel(z_ref, wg_ref, bg_ref, of_ref, ob_ref):
    z = z_ref[...]
    r = lax.broadcasted_iota(jnp.int32, (GLA_CH, GLA_CH), 0)
    c = lax.broadcasted_iota(jnp.int32, (GLA_CH, GLA_CH), 1)
    for d, o_ref in enumerate((of_ref, ob_ref)):
        zz = jnp.dot(z, wg_ref[d], preferred_element_type=F32) + bg_ref[d]
        lg = (jnp.minimum(zz, 0.0) - jnp.log(1.0 + jnp.exp(-jnp.abs(zz)))) * (1.0 / GLA_TAU)
        hi = lg.astype(BF16)
        rest = lg - hi.astype(F32)
        mid = rest.astype(BF16)
        lo = (rest - mid.astype(F32)).astype(BF16)
        tri = jnp.where((r >= c) if d == 0 else (r <= c), 1.0, 0.0).astype(BF16)
        for r0 in range(0, GATE_TM, GLA_CH):
            rows = slice(r0, r0 + GLA_CH)
            o_ref[rows, :] = (jnp.dot(tri, hi[rows, :], preferred_element_type=F32)
                              + jnp.dot(tri, mid[rows, :], preferred_element_type=F32)
                              + jnp.dot(tri, lo[rows, :], preferred_element_type=F32))


def _gla_gates(z, wg_pad, bg):
    tm = GATE_TM
    return pl.pallas_call(
        _gate_kernel,
        grid=(N_TOK // tm,),
        in_specs=[pl.BlockSpec((tm, LANES), lambda i: (i, 0)),
                  pl.BlockSpec((2, LANES, GLA_HK), lambda i: (0, 0, 0)),
                  pl.BlockSpec((2, 1, GLA_HK), lambda i: (0, 0, 0))],
        out_specs=[pl.BlockSpec((tm, GLA_HK), lambda i: (i, 0))] * 2,
        out_shape=[jax.ShapeDtypeStruct((N_TOK, GLA_HK), F32)] * 2,
        compiler_params=_cparams(1, 32),
        name="gla_gate",
    )(z, wg_pad, bg.reshape(2, 1, GLA_HK))


def _gla_kernel(*refs, n_t, t_len, has_h0, want_hfin):
    rest = list(refs)
    qf_ref, kf_ref, vf_ref, bf_ref = (rest.pop(0) for _ in range(4))
    if n_t == 1:
        qb_ref, kb_ref, vb_ref = qf_ref, kf_ref, vf_ref
    else:
        qb_ref, kb_ref, vb_ref = (rest.pop(0) for _ in range(3))
    bb_ref, r_ref, gn_ref = (rest.pop(0) for _ in range(3))
    h0f_ref = rest.pop(0) if has_h0 else None
    h0b_ref = rest.pop(0) if has_h0 else None
    y_ref = rest.pop(0)
    hff_ref = rest.pop(0) if want_hfin else None
    hfb_ref = rest.pop(0) if want_hfin else None
    htf_ref, htb_ref, o_ref = rest
    t = pl.program_id(2)
    L = GLA_CH
    mid = L // 2

    @pl.when(t == 0)
    def _():
        if has_h0:
            htf_ref[...] = h0f_ref[...].T
            htb_ref[...] = h0b_ref[...].T
        else:
            htf_ref[...] = jnp.zeros_like(htf_ref)
            htb_ref[...] = jnp.zeros_like(htb_ref)
        o_ref[...] = jnp.zeros_like(o_ref)

    row = lax.broadcasted_iota(jnp.int32, (L, L), 0)
    col = lax.broadcasted_iota(jnp.int32, (L, L), 1)
    nt = (((1,), (1,)), ((), ()))
    tn = (((0,), (0,)), ((), ()))

    def chunk(q_ref, k_ref, v_ref, b_ref, ht_ref, r0, out0, reverse):
        rows = slice(r0, r0 + L)
        q = q_ref[rows, :].astype(F32) * GLA_DK ** -0.5
        k = k_ref[rows, :].astype(F32)
        v = v_ref[rows, :]
        b = b_ref[rows, :]
        b_end = b[0:1, :] if reverse else b[L - 1:L, :]
        b_mid = b[mid:mid + 1, :] if reverse else b[mid - 1:mid, :]
        qe = (q * jnp.exp(b - b_mid)).astype(BF16)
        ke = (k * jnp.exp(b_mid - b)).astype(BF16)
        a = lax.dot_general(qe, ke, nt, preferred_element_type=F32)
        a = jnp.where((row <= col) if reverse else (row >= col), a, 0.0).astype(BF16)
        ht = ht_ref[...]
        o = lax.dot_general((q * jnp.exp(b)).astype(BF16), ht.astype(BF16), nt, preferred_element_type=F32)
        o = o + jnp.dot(a, v, preferred_element_type=F32)
        o_ref[pl.ds(out0, L), :] += o
        kd = (k * jnp.exp(b_end - b)).astype(BF16)
        ht_ref[...] = ht * jnp.exp(b_end) + lax.dot_general(v, kd, tn, preferred_element_type=F32)

    n_c = GLA_TB // L
    for c in range(n_c):
        chunk(qf_ref, kf_ref, vf_ref, bf_ref, htf_ref, c * L,
              pl.multiple_of(t * GLA_TB + c * L, L), False)
        cb = n_c - 1 - c
        chunk(qb_ref, kb_ref, vb_ref, bb_ref, htb_ref, cb * L,
              pl.multiple_of((n_t - 1 - t) * GLA_TB + cb * L, L), True)

    @pl.when(t == n_t - 1)
    def _():
        for r0 in range(0, t_len, GLA_TB):
            rows = slice(r0, r0 + GLA_TB)
            o = o_ref[rows, :]
            y = o * lax.rsqrt(jnp.mean(o * o, -1, keepdims=True) + EPS) * gn_ref[...]
            y_ref[rows, :] = (y * _silu(r_ref[rows, :].astype(F32))).astype(y_ref.dtype)
        if want_hfin:
            hff_ref[...] = htf_ref[...].T
            hfb_ref[...] = htb_ref[...].T


def _gla(p, bcum, norm_g, h0f, h0b, row0, n_b, t_len, want_hfin, name):
    n_t = t_len // GLA_TB
    blk0 = row0 // GLA_TB
    seq0 = row0 // t_len
    has_h0 = h0f is not None
    k_blk = GLA_HK // GLA_DK
    v_blk = 2 * GLA_HK // GLA_DV
    r_blk = (2 * GLA_HK + GLA_HV) // GLA_DV

    def fwd(b_, t):
        return blk0 + b_ * n_t + t

    def bwd(b_, t):
        return blk0 + b_ * n_t + (n_t - 1 - t)

    def qkv_specs(rb):
        return [pl.BlockSpec((GLA_TB, GLA_DK), lambda b_, h, t: (rb(b_, t), h)),
                pl.BlockSpec((GLA_TB, GLA_DK), lambda b_, h, t: (rb(b_, t), k_blk + h)),
                pl.BlockSpec((GLA_TB, GLA_DV), lambda b_, h, t: (rb(b_, t), v_blk + h))]

    def decay_spec(rb):
        return pl.BlockSpec((GLA_TB, GLA_DK), lambda b_, h, t: (rb(b_, t), h))

    state = pl.BlockSpec((None, None, GLA_DK, GLA_DV), lambda b_, h, t: (b_, h, 0, 0))
    in_specs = qkv_specs(fwd) + [decay_spec(fwd)]
    args = [p, p, p, bcum[0]]
    if n_t > 1:
        in_specs += qkv_specs(bwd)
        args += [p, p, p]
    in_specs += [decay_spec(bwd),
                 pl.BlockSpec((t_len, GLA_DV), lambda b_, h, t: (seq0 + b_, r_blk + h)),
                 pl.BlockSpec((1, GLA_DV), lambda b_, h, t: (0, 0))]
    args += [bcum[1], p, norm_g.reshape(1, GLA_DV)]
    if has_h0:
        in_specs += [state, state]
        args += [h0f, h0b]
    out_specs = [pl.BlockSpec((t_len, GLA_DV), lambda b_, h, t: (b_, h))]
    out_shape = [jax.ShapeDtypeStruct((n_b * t_len, GLA_HV), BF16)]
    if want_hfin:
        out_specs += [state, state]
        out_shape += [jax.ShapeDtypeStruct((n_b, GLA_HEADS, GLA_DK, GLA_DV), F32)] * 2
    return pl.pallas_call(
        functools.partial(_gla_kernel, n_t=n_t, t_len=t_len, has_h0=has_h0, want_hfin=want_hfin),
        grid=(n_b, GLA_HEADS, n_t),
        in_specs=in_specs,
        out_specs=out_specs,
        out_shape=out_shape,
        scratch_shapes=[pltpu.VMEM((GLA_DV, GLA_DK), F32), pltpu.VMEM((GLA_DV, GLA_DK), F32),
                        pltpu.VMEM((t_len, GLA_DV), F32)],
        compiler_params=_cparams(3, 40),
        name=name,
    )(*args)


FFN_TM = 2048
FFN_TN = 256


def _ffn_up_kernel(h_ref, wg_ref, wv_ref, cwg_ref, cwv_ref, cbg_ref, cbv_ref, o_ref):
    tm = FFN_TM
    i = pl.program_id(0)
    seq = jnp.where(i < NP_ROWS // tm, SEQ, DEC_SEQ)
    sub = lax.broadcasted_iota(jnp.int32, (SUBLANES, FFN_TN), 0)
    h = h_ref[...]

    def zero_at_edges(x, first):
        pieces = []
        for t in range(0, tm, SEQ):
            seg = x[t:t + SEQ]
            if first:
                at_edge = (t & (seq - 1)) == 0
                pieces += [jnp.where(at_edge & (sub == 0), 0, seg[:SUBLANES]), seg[SUBLANES:]]
            else:
                at_edge = ((t + SEQ) & (seq - 1)) == 0
                pieces += [seg[:SEQ - SUBLANES],
                           jnp.where(at_edge & (sub == SUBLANES - 1), 0, seg[SEQ - SUBLANES:])]
        return jnp.concatenate(pieces, 0)

    def conv(u, cw_ref, cb_ref):
        up = zero_at_edges(pltpu.roll(u, 1, 0), True)
        un = zero_at_edges(pltpu.roll(u, tm - 1, 0), False)
        return cw_ref[0:1, :] * up + cw_ref[1:2, :] * u + cw_ref[2:3, :] * un + cb_ref[...]

    g = conv(jnp.dot(h, wg_ref[...].astype(BF16), preferred_element_type=F32), cwg_ref, cbg_ref)
    val = conv(jnp.dot(h, wv_ref[...].astype(BF16), preferred_element_type=F32), cwv_ref, cbv_ref)
    o_ref[...] = (_silu(g) * val).astype(o_ref.dtype)


def _ffn_up(h, w_up_all, layer, cw, cb):
    tm, tn = FFN_TM, FFN_TN
    nj = D_FF // tn
    return pl.pallas_call(
        _ffn_up_kernel,
        grid=(N_TOK // tm, nj),
        in_specs=[pl.BlockSpec((tm, D_MODEL), lambda i, j: (i, 0)),
                  pl.BlockSpec((None, D_MODEL, tn), lambda i, j: (layer, 0, j)),
                  pl.BlockSpec((None, D_MODEL, tn), lambda i, j: (layer, 0, nj + j)),
                  pl.BlockSpec((3, tn), lambda i, j: (0, j)),
                  pl.BlockSpec((3, tn), lambda i, j: (0, nj + j)),
                  pl.BlockSpec((1, tn), lambda i, j: (0, j)),
                  pl.BlockSpec((1, tn), lambda i, j: (0, nj + j))],
        out_specs=pl.BlockSpec((tm, tn), lambda i, j: (i, j)),
        out_shape=jax.ShapeDtypeStruct((N_TOK, D_FF), BF16),
        compiler_params=_cparams(2, 48),
        name="ffn_up",
    )(h, w_up_all, w_up_all, cw, cw, cb.reshape(1, -1), cb.reshape(1, -1))


def kernel(x_prompt, x_sample, c, cache_k, cache_v, state_gla_fwd, state_gla_bwd, c_ctx, norm1_g, norm2_g,
           w_ada, b_ada, even_w_in, conv_w, conv_b, conv_ln_g, conv_ln_b, q_norm_g, k_norm_g, even_w_out,
           odd_w_in, gla_w_gate, gla_b_gate, gla_norm_g, odd_w_out, ffn_w_up, ffn_conv_w, ffn_conv_b,
           ffn_w_down, final_norm_g):
    x = (x_prompt.reshape(NP_ROWS, D_MODEL), x_sample.reshape(NS_ROWS, D_MODEL))
    cond = jnp.concatenate([c_ctx[None, :], c, jnp.zeros((N_COND - 1 - DEC_BATCH, D_MODEL), F32)], 0)
    mod_all = _ada(cond, w_ada, b_ada).reshape(DEPTH, N_COND, 6, D_MODEL)
    cos_tab, sin_tab = _rope_tables()
    nkv = N_KV_HEADS * HEAD_DIM

    new_k, new_v, new_sf, new_sb = [], [], [], []
    h = _modnorm(x[0], x[1], norm1_g[0], mod_all[0], 0, 1)
    for l in range(DEPTH):
        mod = mod_all[l]
        mod_norm2 = mod[:, 3:5]
        if l % 2 == 0:
            e = l // 2
            p = _matmul(h, even_w_in[e].astype(BF16), F32, 1024, EVEN_IN // 2, 48, "even_in")
            a = _convmod(p, conv_w[e], conv_b[e], conv_ln_g[e], conv_ln_b[e])
            o_p, k_new, v_new = _attn_ctx(p, q_norm_g[e], k_norm_g[e])
            new_k.append(k_new.reshape(BATCH, SEQ, N_KV_HEADS, HEAD_DIM))
            new_v.append(v_new.reshape(BATCH, SEQ, N_KV_HEADS, HEAD_DIM))
            o_s = _attn_lat(p, cache_k[:, e].reshape(DEC_BATCH, PAST_LEN, nkv),
                            cache_v[:, e].reshape(DEC_BATCH, PAST_LEN, nkv), cos_tab, sin_tab,
                            q_norm_g[e], k_norm_g[e])
            w_out = even_w_out[e].astype(BF16)
            x, h2 = _proj_res_norm([a], [(o_p, o_s)], [w_out[:CONV_CH], w_out[CONV_CH:]], x, mod, 2,
                                   norm2_g[l], mod_norm2, 512, 48, "even_out")
        else:
            od = l // 2
            w_in = odd_w_in[od]
            p = _matmul(h, w_in[:, :GLA_MAIN].astype(BF16), BF16, 1024, 1024, 40, "odd_in")
            w_z = jnp.pad(w_in[:, GLA_MAIN:], ((0, 0), (0, LANES - 2 * GLA_RANK))).astype(BF16)
            z = _matmul(h, w_z, BF16, 1024, LANES, 32, "odd_z")
            wg = gla_w_gate[od]
            wg_pad = jnp.zeros((2, LANES, GLA_HK), F32)
            wg_pad = wg_pad.at[0, :GLA_RANK].set(wg[0]).at[1, GLA_RANK:2 * GLA_RANK].set(wg[1]).astype(BF16)
            bcum = _gla_gates(z, wg_pad, gla_b_gate[od])
            y_p, hf, hb = _gla(p, bcum, gla_norm_g[od], None, None, 0, BATCH, SEQ, True, "gla_ctx")
            new_sf.append(hf)
            new_sb.append(hb)
            (y_s,) = _gla(p, bcum, gla_norm_g[od], state_gla_fwd[:, od], state_gla_bwd[:, od], NP_ROWS,
                          DEC_BATCH, DEC_SEQ, False, "gla_lat")
            x, h2 = _proj_res_norm([], [(y_p, y_s)], [odd_w_out[od].astype(BF16)], x, mod, 2, norm2_g[l],
                                   mod_norm2, 512, 48, "odd_out")
        u = _ffn_up(h2, ffn_w_up, l, ffn_conv_w[l], ffn_conv_b[l])
        w_down = ffn_w_down[l].astype(BF16)
        if l + 1 < DEPTH:
            x, h = _proj_res_norm([u], [], [w_down], x, mod, 5, norm1_g[l + 1], mod_all[l + 1][:, 0:2],
                                  256, 56, "ffn_down")
        else:
            y_p = _proj_res_norm([u], [], [w_down], x, mod, 5, final_norm_g, None, 256, 56, "ffn_down_ctx",
                                 0, NP_ROWS)
            y_s = _proj_res_norm([u], [], [w_down], x, mod, 5, final_norm_g, None, 256, 56, "ffn_down_lat",
                                 NP_ROWS, NS_ROWS)

    y_prompt = y_p.reshape(BATCH, SEQ, D_MODEL)
    y_sample = y_s.reshape(DEC_BATCH, DEC_SEQ, D_MODEL)
    return (y_prompt, y_sample, jnp.stack(new_k, 1), jnp.stack(new_v, 1), jnp.stack(new_sf, 1),
            jnp.stack(new_sb, 1))
```

```python
import functools

import jax
import jax.numpy as jnp
from jax import lax
from jax.experimental import pallas as pl
from jax.experimental.pallas import tpu as pltpu

F32 = jnp.float32
BF16 = jnp.bfloat16

D_MODEL = 2048
BATCH = 32
SEQ = 256
DEPTH = 4
DEC_BATCH = 4
DEC_SEQ = 2048
PAST_LEN = 512
GRID_W = 64
CONV_CH = D_MODEL // 2
CONV_W = 31
HEAD_DIM = 128
N_Q_HEADS = 8
N_KV_HEADS = 2
GQA_GROUP = N_Q_HEADS // N_KV_HEADS
ROPE_THETA = 10000.0
EVEN_IN = 2 * CONV_CH + (N_Q_HEADS + 2 * N_KV_HEADS) * HEAD_DIM
GLA_HEADS = 4
GLA_DK = 256
GLA_DV = 512
GLA_RANK = 16
GLA_TAU = 16.0
GLA_HK = GLA_HEADS * GLA_DK
GLA_HV = GLA_HEADS * GLA_DV
GLA_MAIN = 2 * GLA_HK + 2 * GLA_HV
D_FF = 11 * D_MODEL // 4
EPS = 1e-6

NP_ROWS = BATCH * SEQ
NS_ROWS = DEC_BATCH * DEC_SEQ
N_TOK = NP_ROWS + NS_ROWS
N_COND = 8

LANES = 128
SUBLANES = 8
MIB = 1024 * 1024


def _cparams(n_axes, vmem_mib):
    return pltpu.CompilerParams(dimension_semantics=("arbitrary",) * n_axes,
                                vmem_limit_bytes=vmem_mib * MIB)


def _modrow(i, tm):
    n_ctx = NP_ROWS // tm
    per_seq = DEC_SEQ // tm
    return jnp.where(i < n_ctx, 0, 1 + (i - n_ctx) // per_seq)


def _silu(x):
    return x * jax.nn.sigmoid(x)


def _ada_kernel(c_ref, w_ref, b_ref, o_ref):
    s = _silu(c_ref[...]).astype(BF16)
    o_ref[...] = jnp.dot(s, w_ref[...].astype(BF16), preferred_element_type=F32) + b_ref[...]


def _ada(cond, w_ada, b_ada):
    tn = 1024
    n6 = 6 * D_MODEL
    return pl.pallas_call(
        _ada_kernel,
        grid=(DEPTH, n6 // tn),
        in_specs=[pl.BlockSpec((N_COND, D_MODEL), lambda l, j: (0, 0)),
                  pl.BlockSpec((None, D_MODEL, tn), lambda l, j: (l, 0, j)),
                  pl.BlockSpec((None, 1, tn), lambda l, j: (l, 0, j))],
        out_specs=pl.BlockSpec((None, N_COND, tn), lambda l, j: (l, 0, j)),
        out_shape=jax.ShapeDtypeStruct((DEPTH, N_COND, n6), F32),
        compiler_params=_cparams(2, 40),
        name="ada",
    )(cond, w_ada, b_ada.reshape(DEPTH, 1, n6))


MODNORM_TM = 256


def _modnorm_kernel(xc_ref, xl_ref, g_ref, m_ref, o_ref, *, shift_row, scale_row):
    def body(x_ref):
        x = x_ref[...]
        y = x * lax.rsqrt(jnp.mean(x * x, -1, keepdims=True) + EPS) * g_ref[...]
        y = y * (1.0 + m_ref[scale_row:scale_row + 1, :]) + m_ref[shift_row:shift_row + 1, :]
        o_ref[...] = y.astype(o_ref.dtype)

    i = pl.program_id(0)
    n_ctx = NP_ROWS // MODNORM_TM

    @pl.when(i < n_ctx)
    def _():
        body(xc_ref)

    @pl.when(i >= n_ctx)
    def _():
        body(xl_ref)


def _modnorm(x_ctx, x_lat, g, mod, shift_row, scale_row):
    tm = MODNORM_TM
    n_ctx = NP_ROWS // tm
    return pl.pallas_call(
        functools.partial(_modnorm_kernel, shift_row=shift_row, scale_row=scale_row),
        grid=(N_TOK // tm,),
        in_specs=[pl.BlockSpec((tm, D_MODEL), lambda i: (jnp.minimum(i, n_ctx - 1), 0)),
                  pl.BlockSpec((tm, D_MODEL), lambda i: (jnp.maximum(i - n_ctx, 0), 0)),
                  pl.BlockSpec((1, D_MODEL), lambda i: (0, 0)),
                  pl.BlockSpec((None, 6, D_MODEL), lambda i: (_modrow(i, tm), 0, 0))],
        out_specs=pl.BlockSpec((tm, D_MODEL), lambda i: (i, 0)),
        out_shape=jax.ShapeDtypeStruct((N_TOK, D_MODEL), BF16),
        compiler_params=_cparams(1, 32),
        name="modnorm",
    )(x_ctx, x_lat, g.reshape(1, D_MODEL), mod)


def _mm_kernel(a_ref, w_ref, o_ref):
    o_ref[...] = jnp.dot(a_ref[...], w_ref[...], preferred_element_type=F32).astype(o_ref.dtype)


def _matmul(a, w, out_dtype, tm, tn, vmem_mib, name):
    m, k = a.shape
    n = w.shape[1]
    return pl.pallas_call(
        _mm_kernel,
        grid=(m // tm, n // tn),
        in_specs=[pl.BlockSpec((tm, k), lambda i, j: (i, 0)),
                  pl.BlockSpec((k, tn), lambda i, j: (0, j))],
        out_specs=pl.BlockSpec((tm, tn), lambda i, j: (i, j)),
        out_shape=jax.ShapeDtypeStruct((m, n), out_dtype),
        compiler_params=_cparams(2, vmem_mib),
        name=name,
    )(a, w)


def _proj_kernel(*refs, n_full, n_split, x_split, gate_row, final, tm):
    full_refs = refs[:n_full]
    split_refs = refs[n_full:n_full + 2 * n_split]
    pos = n_full + 2 * n_split
    w_refs = refs[pos:pos + n_full + n_split]
    pos += n_full + n_split
    x_refs = refs[pos:pos + 1 + x_split]
    pos += 1 + x_split
    mg_ref, g_ref = refs[pos:pos + 2]
    pos += 2
    if final:
        (y_ref,) = refs[pos:]
    else:
        mn_ref, xo_ref, h_ref = refs[pos:]
    i = pl.program_id(0)
    half = tm // 2

    def body(lhs_refs, x_ref):
        for r in range(2):
            rows = slice(r * half, (r + 1) * half)
            acc = jnp.dot(lhs_refs[0][rows, :], w_refs[0][...], preferred_element_type=F32)
            for a_ref, w_ref in zip(lhs_refs[1:], w_refs[1:]):
                acc = acc + jnp.dot(a_ref[rows, :], w_ref[...], preferred_element_type=F32)
            xn = x_ref[rows, :] + mg_ref[gate_row:gate_row + 1, :] * acc
            y = xn * lax.rsqrt(jnp.mean(xn * xn, -1, keepdims=True) + EPS) * g_ref[...]
            if final:
                y_ref[rows, :] = y
            else:
                xo_ref[rows, :] = xn
                h_ref[rows, :] = (y * (1.0 + mn_ref[1:2, :]) + mn_ref[0:1, :]).astype(h_ref.dtype)

    if n_split or x_split:
        n_ctx = NP_ROWS // tm

        @pl.when(i < n_ctx)
        def _():
            body(list(full_refs) + list(split_refs[0::2]), x_refs[0])

        @pl.when(i >= n_ctx)
        def _():
            body(list(full_refs) + list(split_refs[1::2]), x_refs[-1])
    else:
        body(list(full_refs), x_refs[0])


def _proj_res_norm(full, split, ws, x, mod_gate, gate_row, norm_g, mod_next, tm, vmem_mib, name,
                   row0=0, n_rows=N_TOK):
    final = mod_next is None
    n_ctx = NP_ROWS // tm
    off = row0 // tm
    x_split = isinstance(x, tuple)
    assert not ((split or x_split) and n_rows != N_TOK)
    ctx_map = lambda i: (jnp.minimum(i, n_ctx - 1), 0)
    lat_map = lambda i: (jnp.maximum(i - n_ctx, 0), 0)
    once = pl.Buffered(1)
    in_specs = [pl.BlockSpec((tm, a.shape[1]), lambda i: (i + off, 0)) for a in full]
    args = list(full)
    for a_ctx, a_lat in split:
        in_specs += [pl.BlockSpec((tm, a_ctx.shape[1]), ctx_map), pl.BlockSpec((tm, a_lat.shape[1]), lat_map)]
        args += [a_ctx, a_lat]
    in_specs += [pl.BlockSpec(w.shape, lambda i: (0, 0), pipeline_mode=once) for w in ws]
    args += list(ws)
    row_spec = pl.BlockSpec((tm, D_MODEL), lambda i: (i, 0))
    if x_split:
        in_specs += [pl.BlockSpec((tm, D_MODEL), ctx_map), pl.BlockSpec((tm, D_MODEL), lat_map)]
        args += list(x)
    else:
        in_specs.append(pl.BlockSpec((tm, D_MODEL), lambda i: (i + off, 0)))
        args.append(x)
    in_specs += [pl.BlockSpec((None, 6, D_MODEL), lambda i: (_modrow(i + off, tm), 0, 0)),
                 pl.BlockSpec((1, D_MODEL), lambda i: (0, 0))]
    args += [mod_gate, norm_g.reshape(1, D_MODEL)]
    if final:
        out_specs = row_spec
        out_shape = jax.ShapeDtypeStruct((n_rows, D_MODEL), F32)
    else:
        in_specs.append(pl.BlockSpec((None, 2, D_MODEL), lambda i: (_modrow(i + off, tm), 0, 0)))
        args.append(mod_next)
        out_specs = [row_spec, row_spec]
        out_shape = [jax.ShapeDtypeStruct((n_rows, D_MODEL), F32),
                     jax.ShapeDtypeStruct((n_rows, D_MODEL), BF16)]
    return pl.pallas_call(
        functools.partial(_proj_kernel, n_full=len(full), n_split=len(split), x_split=int(x_split),
                          gate_row=gate_row, final=final, tm=tm),
        grid=(n_rows // tm,),
        in_specs=in_specs,
        out_specs=out_specs,
        out_shape=out_shape,
        compiler_params=_cparams(1, vmem_mib),
        name=name,
    )(*args)


CONV_TM = 256
CONV_HALO = 16


def _convmod_kernel(a_ref, g_ref, ap_ref, gp_ref, an_ref, gn_ref, cw_ref, cb_ref, lng_ref, lnb_ref,
                    o_ref, ext_ref, y_ref, win_ref):
    tm = CONV_TM
    i = pl.program_id(0)
    n_ctx = NP_ROWS // tm
    per_seq = DEC_SEQ // tm
    j = (i - n_ctx) % per_seq
    latent = i >= n_ctx
    has_prev = jnp.logical_and(latent, j != 0)
    has_next = jnp.logical_and(latent, j != per_seq - 1)

    def glu(a, g):
        return a * jax.nn.sigmoid(g)

    ext_ref[CONV_HALO:CONV_HALO + tm, :] = glu(a_ref[...], g_ref[...])
    ext_ref[0:CONV_HALO, :] = jnp.where(has_prev, glu(ap_ref[...], gp_ref[...]), 0.0)
    ext_ref[CONV_HALO + tm:, :] = jnp.where(has_next, glu(an_ref[...], gn_ref[...]), 0.0)

    off = CONV_HALO - CONV_W // 2
    span = tm + SUBLANES * ((off + CONV_W - 1) // SUBLANES)
    hrows = tm // 2
    for cb in range(CONV_CH // LANES):
        cs = slice(cb * LANES, (cb + 1) * LANES)
        for r in range(SUBLANES):
            win_ref[r, 0:span, :] = ext_ref[r:r + span, cs]
        for base in range(0, tm, hrows):
            acc = jnp.zeros((hrows, LANES), F32) + cb_ref[:, cs]
            for k in range(CONV_W):
                r = (off + k) % SUBLANES
                m = (off + k) // SUBLANES * SUBLANES
                acc = acc + cw_ref[k:k + 1, cs] * win_ref[r, base + m:base + m + hrows, :]
            y_ref[base:base + hrows, cs] = acc

    y = y_ref[...]
    mu = jnp.mean(y, -1, keepdims=True)
    yc = y - mu
    z = yc * lax.rsqrt(jnp.mean(yc * yc, -1, keepdims=True) + EPS) * lng_ref[...] + lnb_ref[...]
    o_ref[...] = _silu(z).astype(o_ref.dtype)


def _convmod(p, cw, cb, ln_g, ln_b):
    tm = CONV_TM
    hb = tm // CONV_HALO
    last = N_TOK // CONV_HALO - 1
    prev_map = lambda i: (jnp.maximum(i * hb - 1, 0), 0)
    prev_map_g = lambda i: (jnp.maximum(i * hb - 1, 0), 1)
    next_map = lambda i: (jnp.minimum((i + 1) * hb, last), 0)
    next_map_g = lambda i: (jnp.minimum((i + 1) * hb, last), 1)
    vec = pl.BlockSpec((1, CONV_CH), lambda i: (0, 0))
    return pl.pallas_call(
        _convmod_kernel,
        grid=(N_TOK // tm,),
        in_specs=[pl.BlockSpec((tm, CONV_CH), lambda i: (i, 0)),
                  pl.BlockSpec((tm, CONV_CH), lambda i: (i, 1)),
                  pl.BlockSpec((CONV_HALO, CONV_CH), prev_map),
                  pl.BlockSpec((CONV_HALO, CONV_CH), prev_map_g),
                  pl.BlockSpec((CONV_HALO, CONV_CH), next_map),
                  pl.BlockSpec((CONV_HALO, CONV_CH), next_map_g),
                  pl.BlockSpec((CONV_W, CONV_CH), lambda i: (0, 0)),
                  vec, vec, vec],
        out_specs=pl.BlockSpec((tm, CONV_CH), lambda i: (i, 0)),
        out_shape=jax.ShapeDtypeStruct((N_TOK, CONV_CH), BF16),
        scratch_shapes=[pltpu.VMEM((tm + 2 * CONV_HALO, CONV_CH), F32),
                        pltpu.VMEM((tm, CONV_CH), F32),
                        pltpu.VMEM((SUBLANES, tm + 2 * CONV_HALO, LANES), F32)],
        compiler_params=_cparams(1, 32),
        name="convmod",
    )(p, p, p, p, p, p, cw, cb.reshape(1, -1), ln_g.reshape(1, -1), ln_b.reshape(1, -1))


ATT_GW = GQA_GROUP * HEAD_DIM
ATT_Q_BLK = 2 * CONV_CH // ATT_GW
ATT_K_BLK = (2 * CONV_CH + N_Q_HEADS * HEAD_DIM) // HEAD_DIM
ATT_V_BLK = ATT_K_BLK + N_KV_HEADS
ATT_TQ = 256


def _rms_rows(x, g):
    return x * lax.rsqrt(jnp.mean(x * x, -1, keepdims=True) + EPS) * g


def _rope(x, cos, sin):
    lane = lax.broadcasted_iota(jnp.int32, x.shape, 1)
    partner = jnp.where((lane & 1) == 0, pltpu.roll(x, HEAD_DIM - 1, 1), pltpu.roll(x, 1, 1))
    return x * cos + partner * sin


def _attend_group(q_ref, qg_ref, rotate, k, v, o_ref):
    for g in range(GQA_GROUP):
        hs = slice(g * HEAD_DIM, (g + 1) * HEAD_DIM)
        q = _rms_rows(q_ref[:, hs], qg_ref[...])
        if rotate is not None:
            q = rotate(q)
        q = (q * HEAD_DIM ** -0.5).astype(BF16)
        s = lax.dot_general(q, k, (((1,), (1,)), ((), ())), preferred_element_type=F32)
        p = jnp.exp(s - jnp.max(s, -1, keepdims=True))
        denom = jnp.sum(p, -1, keepdims=True)
        o = jnp.dot(p.astype(BF16), v, preferred_element_type=F32) / denom
        o_ref[:, hs] = o.astype(o_ref.dtype)


def _attn_ctx_kernel(q_ref, k_ref, v_ref, qg_ref, kg_ref, o_ref, kf_ref, vf_ref):
    kn = _rms_rows(k_ref[...], kg_ref[...])
    v = v_ref[...]
    kf_ref[...] = kn
    vf_ref[...] = v
    _attend_group(q_ref, qg_ref, None, kn.astype(BF16), v.astype(BF16), o_ref)


def _attn_ctx(p, qg, kg):
    nkv = N_KV_HEADS * HEAD_DIM
    vec = pl.BlockSpec((1, HEAD_DIM), lambda b, h: (0, 0))
    kv_out = pl.BlockSpec((SEQ, HEAD_DIM), lambda b, h: (b, h))
    return pl.pallas_call(
        _attn_ctx_kernel,
        grid=(BATCH, N_KV_HEADS),
        in_specs=[pl.BlockSpec((SEQ, ATT_GW), lambda b, h: (b, ATT_Q_BLK + h)),
                  pl.BlockSpec((SEQ, HEAD_DIM), lambda b, h: (b, ATT_K_BLK + h)),
                  pl.BlockSpec((SEQ, HEAD_DIM), lambda b, h: (b, ATT_V_BLK + h)),
                  vec, vec],
        out_specs=[pl.BlockSpec((SEQ, ATT_GW), lambda b, h: (b, h)), kv_out, kv_out],
        out_shape=[jax.ShapeDtypeStruct((NP_ROWS, N_Q_HEADS * HEAD_DIM), BF16),
                   jax.ShapeDtypeStruct((NP_ROWS, nkv), F32),
                   jax.ShapeDtypeStruct((NP_ROWS, nkv), F32)],
        compiler_params=_cparams(2, 32),
        name="attn_ctx",
    )(p, p, p, qg.reshape(1, -1), kg.reshape(1, -1))


def _attn_lat_kernel(q_ref, k_ref, v_ref, ck_ref, cv_ref, cosq_ref, sinq_ref, cosk_ref, sink_ref,
                     qg_ref, kg_ref, o_ref, ks_ref, vs_ref):
    @pl.when(pl.program_id(2) == 0)
    def _():
        for r0 in range(0, DEC_SEQ, ATT_TQ):
            rows = slice(r0, r0 + ATT_TQ)
            kn = _rms_rows(k_ref[rows, :], kg_ref[...])
            ks_ref[rows, :] = _rope(kn, cosk_ref[rows, :], sink_ref[rows, :]).astype(BF16)
            vs_ref[rows, :] = v_ref[rows, :].astype(BF16)
        ks_ref[DEC_SEQ:, :] = ck_ref[...].astype(BF16)
        vs_ref[DEC_SEQ:, :] = cv_ref[...].astype(BF16)

    cos = cosq_ref[...]
    sin = sinq_ref[...]
    _attend_group(q_ref, qg_ref, lambda x: _rope(x, cos, sin), ks_ref[...], vs_ref[...], o_ref)


def _attn_lat(p, cache_k, cache_v, cos_tab, sin_tab, qg, kg):
    tq = ATT_TQ
    n_q = DEC_SEQ // tq
    row0 = NP_ROWS // tq
    seq0 = NP_ROWS // DEC_SEQ
    vec = pl.BlockSpec((1, HEAD_DIM), lambda b, h, i: (0, 0))
    cache = pl.BlockSpec((None, PAST_LEN, HEAD_DIM), lambda b, h, i: (b, 0, h))
    tab_q = pl.BlockSpec((tq, HEAD_DIM), lambda b, h, i: (i, 0))
    tab_k = pl.BlockSpec((DEC_SEQ, HEAD_DIM), lambda b, h, i: (0, 0))
    return pl.pallas_call(
        _attn_lat_kernel,
        grid=(DEC_BATCH, N_KV_HEADS, n_q),
        in_specs=[pl.BlockSpec((tq, ATT_GW), lambda b, h, i: (row0 + b * n_q + i, ATT_Q_BLK + h)),
                  pl.BlockSpec((DEC_SEQ, HEAD_DIM), lambda b, h, i: (seq0 + b, ATT_K_BLK + h)),
                  pl.BlockSpec((DEC_SEQ, HEAD_DIM), lambda b, h, i: (seq0 + b, ATT_V_BLK + h)),
                  cache, cache, tab_q, tab_q, tab_k, tab_k, vec, vec],
        out_specs=pl.BlockSpec((tq, ATT_GW), lambda b, h, i: (b * n_q + i, h)),
        out_shape=jax.ShapeDtypeStruct((NS_ROWS, N_Q_HEADS * HEAD_DIM), BF16),
        scratch_shapes=[pltpu.VMEM((DEC_SEQ + PAST_LEN, HEAD_DIM), BF16),
                        pltpu.VMEM((DEC_SEQ + PAST_LEN, HEAD_DIM), BF16)],
        compiler_params=_cparams(3, 40),
        name="attn_lat",
    )(p, p, p, cache_k, cache_v, cos_tab, sin_tab, cos_tab, sin_tab, qg.reshape(1, -1), kg.reshape(1, -1))


def _rope_tables():
    rows = DEC_SEQ // GRID_W
    row = jnp.repeat(jnp.arange(rows), GRID_W).astype(F32)
    col = jnp.tile(jnp.arange(GRID_W), rows).astype(F32)
    n = HEAD_DIM // 4
    inv = ROPE_THETA ** (-jnp.arange(n, dtype=F32) / n)
    ang = jnp.concatenate([row[:, None] * inv, col[:, None] * inv], -1)
    cos = jnp.repeat(jnp.cos(ang), 2, axis=-1)
    sin = jnp.repeat(jnp.sin(ang), 2, axis=-1) * jnp.tile(jnp.array([-1.0, 1.0], F32), HEAD_DIM // 2)
    return cos, sin


GLA_CH = 128
GATE_TM = 512
GLA_TB = 256


def _gate_kernel(z_ref, wg_ref, bg_ref, of_ref, ob_ref):
    z = z_ref[...]
    r = lax.broadcasted_iota(jnp.int32, (GLA_CH, GLA_CH), 0)
    c = lax.broadcasted_iota(jnp.int32, (GLA_CH, GLA_CH), 1)
    for d, o_ref in enumerate((of_ref, ob_ref)):
        zz = jnp.dot(z, wg_ref[d], preferred_element_type=F32) + bg_ref[d]
        lg = (jnp.minimum(zz, 0.0) - jnp.log(1.0 + jnp.exp(-jnp.abs(zz)))) * (1.0 / GLA_TAU)
        hi = lg.astype(BF16)
        rest = lg - hi.astype(F32)
        mid = rest.astype(BF16)
        lo = (rest - mid.astype(F32)).astype(BF16)
        tri = jnp.where((r >= c) if d == 0 else (r <= c), 1.0, 0.0).astype(BF16)
        for r0 in range(0, GATE_TM, GLA_CH):
            rows = slice(r0, r0 + GLA_CH)
            o_ref[rows, :] = (jnp.dot(tri, hi[rows, :], preferred_element_type=F32)
                              + jnp.dot(tri, mid[rows, :], preferred_element_type=F32)
                              + jnp.dot(tri, lo[rows, :], preferred_element_type=F32))


def _gla_gates(z, wg_pad, bg):
    tm = GATE_TM
    return pl.pallas_call(
        _gate_kernel,
        grid=(N_TOK // tm,),
        in_specs=[pl.BlockSpec((tm, LANES), lambda i: (i, 0)),
                  pl.BlockSpec((2, LANES, GLA_HK), lambda i: (0, 0, 0)),
                  pl.BlockSpec((2, 1, GLA_HK), lambda i: (0, 0, 0))],
        out_specs=[pl.BlockSpec((tm, GLA_HK), lambda i: (i, 0))] * 2,
        out_shape=[jax.ShapeDtypeStruct((N_TOK, GLA_HK), F32)] * 2,
        compiler_params=_cparams(1, 32),
        name="gla_gate",
    )(z, wg_pad, bg.reshape(2, 1, GLA_HK))


def _gla_kernel(*refs, n_t, t_len, has_h0, want_hfin):
    rest = list(refs)
    qf_ref, kf_ref, vf_ref, bf_ref = (rest.pop(0) for _ in range(4))
    if n_t == 1:
        qb_ref, kb_ref, vb_ref = qf_ref, kf_ref, vf_ref
    else:
        qb_ref, kb_ref, vb_ref = (rest.pop(0) for _ in range(3))
    bb_ref, r_ref, gn_ref = (rest.pop(0) for _ in range(3))
    h0f_ref = rest.pop(0) if has_h0 else None
    h0b_ref = rest.pop(0) if has_h0 else None
    y_ref = rest.pop(0)
    hff_ref = rest.pop(0) if want_hfin else None
    hfb_ref = rest.pop(0) if want_hfin else None
    htf_ref, htb_ref, o_ref = rest
    t = pl.program_id(2)
    L = GLA_CH
    mid = L // 2

    @pl.when(t == 0)
    def _():
        if has_h0:
            htf_ref[...] = h0f_ref[...].T
            htb_ref[...] = h0b_ref[...].T
        else:
            htf_ref[...] = jnp.zeros_like(htf_ref)
            htb_ref[...] = jnp.zeros_like(htb_ref)
        o_ref[...] = jnp.zeros_like(o_ref)

    row = lax.broadcasted_iota(jnp.int32, (L, L), 0)
    col = lax.broadcasted_iota(jnp.int32, (L, L), 1)
    nt = (((1,), (1,)), ((), ()))
    tn = (((0,), (0,)), ((), ()))

    def chunk(q_ref, k_ref, v_ref, b_ref, ht_ref, r0, out0, reverse):
        rows = slice(r0, r0 + L)
        q = q_ref[rows, :].astype(F32) * GLA_DK ** -0.5
        k = k_ref[rows, :].astype(F32)
        v = v_ref[rows, :]
        b = b_ref[rows, :]
        b_end = b[0:1, :] if reverse else b[L - 1:L, :]
        b_mid = b[mid:mid + 1, :] if reverse else b[mid - 1:mid, :]
        qe = (q * jnp.exp(b - b_mid)).astype(BF16)
        ke = (k * jnp.exp(b_mid - b)).astype(BF16)
        a = lax.dot_general(qe, ke, nt, preferred_element_type=F32)
        a = jnp.where((row <= col) if reverse else (row >= col), a, 0.0).astype(BF16)
        ht = ht_ref[...]
        o = lax.dot_general((q * jnp.exp(b)).astype(BF16), ht.astype(BF16), nt, preferred_element_type=F32)
        o = o + jnp.dot(a, v, preferred_element_type=F32)
        o_ref[pl.ds(out0, L), :] += o
        kd = (k * jnp.exp(b_end - b)).astype(BF16)
        ht_ref[...] = ht * jnp.exp(b_end) + lax.dot_general(v, kd, tn, preferred_element_type=F32)

    n_c = GLA_TB // L
    for c in range(n_c):
        chunk(qf_ref, kf_ref, vf_ref, bf_ref, htf_ref, c * L,
              pl.multiple_of(t * GLA_TB + c * L, L), False)
        cb = n_c - 1 - c
        chunk(qb_ref, kb_ref, vb_ref, bb_ref, htb_ref, cb * L,
              pl.multiple_of((n_t - 1 - t) * GLA_TB + cb * L, L), True)

    @pl.when(t == n_t - 1)
    def _():
        for r0 in range(0, t_len, GLA_TB):
            rows = slice(r0, r0 + GLA_TB)
            o = o_ref[rows, :]
            y = o * lax.rsqrt(jnp.mean(o * o, -1, keepdims=True) + EPS) * gn_ref[...]
            y_ref[rows, :] = (y * _silu(r_ref[rows, :].astype(F32))).astype(y_ref.dtype)
        if want_hfin:
            hff_ref[...] = htf_ref[...].T
            hfb_ref[...] = htb_ref[...].T


def _gla(p, bcum, norm_g, h0f, h0b, row0, n_b, t_len, want_hfin, name):
    n_t = t_len // GLA_TB
    blk0 = row0 // GLA_TB
    seq0 = row0 // t_len
    has_h0 = h0f is not None
    k_blk = GLA_HK // GLA_DK
    v_blk = 2 * GLA_HK // GLA_DV
    r_blk = (2 * GLA_HK + GLA_HV) // GLA_DV

    def fwd(b_, t):
        return blk0 + b_ * n_t + t

    def bwd(b_, t):
        return blk0 + b_ * n_t + (n_t - 1 - t)

    def qkv_specs(rb):
        return [pl.BlockSpec((GLA_TB, GLA_DK), lambda b_, h, t: (rb(b_, t), h)),
                pl.BlockSpec((GLA_TB, GLA_DK), lambda b_, h, t: (rb(b_, t), k_blk + h)),
                pl.BlockSpec((GLA_TB, GLA_DV), lambda b_, h, t: (rb(b_, t), v_blk + h))]

    def decay_spec(rb):
        return pl.BlockSpec((GLA_TB, GLA_DK), lambda b_, h, t: (rb(b_, t), h))

    state = pl.BlockSpec((None, None, GLA_DK, GLA_DV), lambda b_, h, t: (b_, h, 0, 0))
    in_specs = qkv_specs(fwd) + [decay_spec(fwd)]
    args = [p, p, p, bcum[0]]
    if n_t > 1:
        in_specs += qkv_specs(bwd)
        args += [p, p, p]
    in_specs += [decay_spec(bwd),
                 pl.BlockSpec((t_len, GLA_DV), lambda b_, h, t: (seq0 + b_, r_blk + h)),
                 pl.BlockSpec((1, GLA_DV), lambda b_, h, t: (0, 0))]
    args += [bcum[1], p, norm_g.reshape(1, GLA_DV)]
    if has_h0:
        in_specs += [state, state]
        args += [h0f, h0b]
    out_specs = [pl.BlockSpec((t_len, GLA_DV), lambda b_, h, t: (b_, h))]
    out_shape = [jax.ShapeDtypeStruct((n_b * t_len, GLA_HV), BF16)]
    if want_hfin:
        out_specs += [state, state]
        out_shape += [jax.ShapeDtypeStruct((n_b, GLA_HEADS, GLA_DK, GLA_DV), F32)] * 2
    return pl.pallas_call(
        functools.partial(_gla_kernel, n_t=n_t, t_len=t_len, has_h0=has_h0, want_hfin=want_hfin),
        grid=(n_b, GLA_HEADS, n_t),
        in_specs=in_specs,
        out_specs=out_specs,
        out_shape=out_shape,
        scratch_shapes=[pltpu.VMEM((GLA_DV, GLA_DK), F32), pltpu.VMEM((GLA_DV, GLA_DK), F32),
                        pltpu.VMEM((t_len, GLA_DV), F32)],
        compiler_params=_cparams(3, 40),
        name=name,
    )(*args)


FFN_TM = 2048
FFN_TN = 256


def _ffn_up_kernel(h_ref, wg_ref, wv_ref, cwg_ref, cwv_ref, cbg_ref, cbv_ref, o_ref):
    tm = FFN_TM
    i = pl.program_id(0)
    seq = jnp.where(i < NP_ROWS // tm, SEQ, DEC_SEQ)
    sub = lax.broadcasted_iota(jnp.int32, (SUBLANES, FFN_TN), 0)
    h = h_ref[...]

    def zero_at_edges(x, first):
        pieces = []
        for t in range(0, tm, SEQ):
            seg = x[t:t + SEQ]
            if first:
                at_edge = (t & (seq - 1)) == 0
                pieces += [jnp.where(at_edge & (sub == 0), 0, seg[:SUBLANES]), seg[SUBLANES:]]
            else:
                at_edge = ((t + SEQ) & (seq - 1)) == 0
                pieces += [seg[:SEQ - SUBLANES],
                           jnp.where(at_edge & (sub == SUBLANES - 1), 0, seg[SEQ - SUBLANES:])]
        return jnp.concatenate(pieces, 0)

    def pair(g, v):
        return pltpu.pack_elementwise([g, v], packed_dtype=BF16)

    def as_bf16(words):
        return pltpu.bitcast(words, BF16)

    def per_channel(g_row, v_row):
        words = pair(jnp.broadcast_to(g_row, (SUBLANES, FFN_TN)), jnp.broadcast_to(v_row, (SUBLANES, FFN_TN)))
        return jnp.tile(as_bf16(words), (tm // SUBLANES, 1))

    u = pair(jnp.dot(h, wg_ref[...].astype(BF16), preferred_element_type=F32),
             jnp.dot(h, wv_ref[...].astype(BF16), preferred_element_type=F32))
    up = zero_at_edges(pltpu.roll(u, 1, 0), True)
    un = zero_at_edges(pltpu.roll(u, tm - 1, 0), False)
    y = (per_channel(cwg_ref[0:1, :], cwv_ref[0:1, :]) * as_bf16(up)
         + per_channel(cwg_ref[1:2, :], cwv_ref[1:2, :]) * as_bf16(u)
         + per_channel(cwg_ref[2:3, :], cwv_ref[2:3, :]) * as_bf16(un)
         + per_channel(cbg_ref[...], cbv_ref[...]))
    y = pltpu.bitcast(y, jnp.int32)
    g = pltpu.unpack_elementwise(y, index=0, packed_dtype=BF16, unpacked_dtype=F32)
    val = pltpu.unpack_elementwise(y, index=1, packed_dtype=BF16, unpacked_dtype=F32)
    o_ref[...] = (_silu(g) * val).astype(o_ref.dtype)


def _ffn_up(h, w_up_all, layer, cw, cb):
    tm, tn = FFN_TM, FFN_TN
    nj = D_FF // tn
    return pl.pallas_call(
        _ffn_up_kernel,
        grid=(N_TOK // tm, nj),
        in_specs=[pl.BlockSpec((tm, D_MODEL), lambda i, j: (i, 0)),
                  pl.BlockSpec((None, D_MODEL, tn), lambda i, j: (layer, 0, j)),
                  pl.BlockSpec((None, D_MODEL, tn), lambda i, j: (layer, 0, nj + j)),
                  pl.BlockSpec((3, tn), lambda i, j: (0, j)),
                  pl.BlockSpec((3, tn), lambda i, j: (0, nj + j)),
                  pl.BlockSpec((1, tn), lambda i, j: (0, j)),
                  pl.BlockSpec((1, tn), lambda i, j: (0, nj + j))],
        out_specs=pl.BlockSpec((tm, tn), lambda i, j: (i, j)),
        out_shape=jax.ShapeDtypeStruct((N_TOK, D_FF), BF16),
        compiler_params=_cparams(2, 48),
        name="ffn_up",
    )(h, w_up_all, w_up_all, cw, cw, cb.reshape(1, -1), cb.reshape(1, -1))


def kernel(x_prompt, x_sample, c, cache_k, cache_v, state_gla_fwd, state_gla_bwd, c_ctx, norm1_g, norm2_g,
           w_ada, b_ada, even_w_in, conv_w, conv_b, conv_ln_g, conv_ln_b, q_norm_g, k_norm_g, even_w_out,
           odd_w_in, gla_w_gate, gla_b_gate, gla_norm_g, odd_w_out, ffn_w_up, ffn_conv_w, ffn_conv_b,
           ffn_w_down, final_norm_g):
    x = (x_prompt.reshape(NP_ROWS, D_MODEL), x_sample.reshape(NS_ROWS, D_MODEL))
    cond = jnp.concatenate([c_ctx[None, :], c, jnp.zeros((N_COND - 1 - DEC_BATCH, D_MODEL), F32)], 0)
    mod_all = _ada(cond, w_ada, b_ada).reshape(DEPTH, N_COND, 6, D_MODEL)
    cos_tab, sin_tab = _rope_tables()
    nkv = N_KV_HEADS * HEAD_DIM

    new_k, new_v, new_sf, new_sb = [], [], [], []
    h = _modnorm(x[0], x[1], norm1_g[0], mod_all[0], 0, 1)
    for l in range(DEPTH):
        mod = mod_all[l]
        mod_norm2 = mod[:, 3:5]
        if l % 2 == 0:
            e = l // 2
            p = _matmul(h, even_w_in[e].astype(BF16), F32, 1024, EVEN_IN // 2, 48, "even_in")
            a = _convmod(p, conv_w[e], conv_b[e], conv_ln_g[e], conv_ln_b[e])
            o_p, k_new, v_new = _attn_ctx(p, q_norm_g[e], k_norm_g[e])
            new_k.append(k_new.reshape(BATCH, SEQ, N_KV_HEADS, HEAD_DIM))
            new_v.append(v_new.reshape(BATCH, SEQ, N_KV_HEADS, HEAD_DIM))
            o_s = _attn_lat(p, cache_k[:, e].reshape(DEC_BATCH, PAST_LEN, nkv),
                            cache_v[:, e].reshape(DEC_BATCH, PAST_LEN, nkv), cos_tab, sin_tab,
                            q_norm_g[e], k_norm_g[e])
            w_out = even_w_out[e].astype(BF16)
            x, h2 = _proj_res_norm([a], [(o_p, o_s)], [w_out[:CONV_CH], w_out[CONV_CH:]], x, mod, 2,
                                   norm2_g[l], mod_norm2, 512, 48, "even_out")
        else:
            od = l // 2
            w_in = odd_w_in[od]
            p = _matmul(h, w_in[:, :GLA_MAIN].astype(BF16), BF16, 1024, 1024, 40, "odd_in")
            w_z = jnp.pad(w_in[:, GLA_MAIN:], ((0, 0), (0, LANES - 2 * GLA_RANK))).astype(BF16)
            z = _matmul(h, w_z, BF16, 1024, LANES, 32, "odd_z")
            wg = gla_w_gate[od]
            wg_pad = jnp.zeros((2, LANES, GLA_HK), F32)
            wg_pad = wg_pad.at[0, :GLA_RANK].set(wg[0]).at[1, GLA_RANK:2 * GLA_RANK].set(wg[1]).astype(BF16)
            bcum = _gla_gates(z, wg_pad, gla_b_gate[od])
            y_p, hf, hb = _gla(p, bcum, gla_norm_g[od], None, None, 0, BATCH, SEQ, True, "gla_ctx")
            new_sf.append(hf)
            new_sb.append(hb)
            (y_s,) = _gla(p, bcum, gla_norm_g[od], state_gla_fwd[:, od], state_gla_bwd[:, od], NP_ROWS,
                          DEC_BATCH, DEC_SEQ, False, "gla_lat")
            x, h2 = _proj_res_norm([], [(y_p, y_s)], [odd_w_out[od].astype(BF16)], x, mod, 2, norm2_g[l],
                                   mod_norm2, 512, 48, "odd_out")
        u = _ffn_up(h2, ffn_w_up, l, ffn_conv_w[l], ffn_conv_b[l])
        w_down = ffn_w_down[l].astype(BF16)
        if l + 1 < DEPTH:
            x, h = _proj_res_norm([u], [], [w_down], x, mod, 5, norm1_g[l + 1], mod_all[l + 1][:, 0:2],
                                  256, 56, "ffn_down")
        else:
            y_p = _proj_res_norm([u], [], [w_down], x, mod, 5, final_norm_g, None, 256, 56, "ffn_down_ctx",
                                 0, NP_ROWS)
            y_s = _proj_res_norm([u], [], [w_down], x, mod, 5, final_norm_g, None, 256, 56, "ffn_down_lat",
                                 NP_ROWS, NS_ROWS)

    y_prompt = y_p.reshape(BATCH, SEQ, D_MODEL)
    y_sample = y_s.reshape(DEC_BATCH, DEC_SEQ, D_MODEL)
    return (y_prompt, y_sample, jnp.stack(new_k, 1), jnp.stack(new_v, 1), jnp.stack(new_sf, 1),
            jnp.stack(new_sb, 1))
```

```python
import functools

import jax
import jax.numpy as jnp
from jax import lax
from jax.experimental import pallas as pl
from jax.experimental.pallas import tpu as pltpu

F32 = jnp.float32
BF16 = jnp.bfloat16

D_MODEL = 2048
BATCH = 32
SEQ = 256
DEPTH = 4
DEC_BATCH = 4
DEC_SEQ = 2048
PAST_LEN = 512
GRID_W = 64
CONV_CH = D_MODEL // 2
CONV_W = 31
HEAD_DIM = 128
N_Q_HEADS = 8
N_KV_HEADS = 2
GQA_GROUP = N_Q_HEADS // N_KV_HEADS
ROPE_THETA = 10000.0
EVEN_IN = 2 * CONV_CH + (N_Q_HEADS + 2 * N_KV_HEADS) * HEAD_DIM
GLA_HEADS = 4
GLA_DK = 256
GLA_DV = 512
GLA_RANK = 16
GLA_TAU = 16.0
GLA_HK = GLA_HEADS * GLA_DK
GLA_HV = GLA_HEADS * GLA_DV
GLA_MAIN = 2 * GLA_HK + 2 * GLA_HV
N_ODD = DEPTH // 2
D_FF = 11 * D_MODEL // 4
EPS = 1e-6

NP_ROWS = BATCH * SEQ
NS_ROWS = DEC_BATCH * DEC_SEQ
N_TOK = NP_ROWS + NS_ROWS
N_COND = 8

LANES = 128
SUBLANES = 8
MIB = 1024 * 1024


def _cparams(n_axes, vmem_mib):
    return pltpu.CompilerParams(dimension_semantics=("arbitrary",) * n_axes,
                                vmem_limit_bytes=vmem_mib * MIB)


def _modrow(i, tm):
    n_ctx = NP_ROWS // tm
    per_seq = DEC_SEQ // tm
    return jnp.where(i < n_ctx, 0, 1 + (i - n_ctx) // per_seq)


def _silu(x):
    return x * jax.nn.sigmoid(x)


def _ada_kernel(c_ref, w_ref, b_ref, o_ref):
    s = _silu(c_ref[...]).astype(BF16)
    o_ref[...] = jnp.dot(s, w_ref[...].astype(BF16), preferred_element_type=F32) + b_ref[...]


def _ada(cond, w_ada, b_ada):
    tn = 1024
    n6 = 6 * D_MODEL
    return pl.pallas_call(
        _ada_kernel,
        grid=(DEPTH, n6 // tn),
        in_specs=[pl.BlockSpec((N_COND, D_MODEL), lambda l, j: (0, 0)),
                  pl.BlockSpec((None, D_MODEL, tn), lambda l, j: (l, 0, j)),
                  pl.BlockSpec((None, 1, tn), lambda l, j: (l, 0, j))],
        out_specs=pl.BlockSpec((None, N_COND, tn), lambda l, j: (l, 0, j)),
        out_shape=jax.ShapeDtypeStruct((DEPTH, N_COND, n6), F32),
        compiler_params=_cparams(2, 40),
        name="ada",
    )(cond, w_ada, b_ada.reshape(DEPTH, 1, n6))


MODNORM_TM = 256


def _modnorm_kernel(xc_ref, xl_ref, g_ref, m_ref, o_ref, *, shift_row, scale_row):
    def body(x_ref):
        x = x_ref[...]
        y = x * lax.rsqrt(jnp.mean(x * x, -1, keepdims=True) + EPS) * g_ref[...]
        y = y * (1.0 + m_ref[scale_row:scale_row + 1, :]) + m_ref[shift_row:shift_row + 1, :]
        o_ref[...] = y.astype(o_ref.dtype)

    i = pl.program_id(0)
    n_ctx = NP_ROWS // MODNORM_TM

    @pl.when(i < n_ctx)
    def _():
        body(xc_ref)

    @pl.when(i >= n_ctx)
    def _():
        body(xl_ref)


def _modnorm(x_ctx, x_lat, g, mod, shift_row, scale_row):
    tm = MODNORM_TM
    n_ctx = NP_ROWS // tm
    return pl.pallas_call(
        functools.partial(_modnorm_kernel, shift_row=shift_row, scale_row=scale_row),
        grid=(N_TOK // tm,),
        in_specs=[pl.BlockSpec((tm, D_MODEL), lambda i: (jnp.minimum(i, n_ctx - 1), 0)),
                  pl.BlockSpec((tm, D_MODEL), lambda i: (jnp.maximum(i - n_ctx, 0), 0)),
                  pl.BlockSpec((1, D_MODEL), lambda i: (0, 0)),
                  pl.BlockSpec((None, 6, D_MODEL), lambda i: (_modrow(i, tm), 0, 0))],
        out_specs=pl.BlockSpec((tm, D_MODEL), lambda i: (i, 0)),
        out_shape=jax.ShapeDtypeStruct((N_TOK, D_MODEL), BF16),
        compiler_params=_cparams(1, 32),
        name="modnorm",
    )(x_ctx, x_lat, g.reshape(1, D_MODEL), mod)


def _mm_kernel(a_ref, w_ref, o_ref):
    o_ref[...] = jnp.dot(a_ref[...], w_ref[...], preferred_element_type=F32).astype(o_ref.dtype)


def _matmul(a, w, out_dtype, tm, tn, vmem_mib, name):
    m, k = a.shape
    n = w.shape[1]
    return pl.pallas_call(
        _mm_kernel,
        grid=(m // tm, n // tn),
        in_specs=[pl.BlockSpec((tm, k), lambda i, j: (i, 0)),
                  pl.BlockSpec((k, tn), lambda i, j: (0, j))],
        out_specs=pl.BlockSpec((tm, tn), lambda i, j: (i, j)),
        out_shape=jax.ShapeDtypeStruct((m, n), out_dtype),
        compiler_params=_cparams(2, vmem_mib),
        name=name,
    )(a, w)


def _proj_kernel(*refs, n_full, n_split, x_split, gate_row, final, tm):
    full_refs = refs[:n_full]
    split_refs = refs[n_full:n_full + 2 * n_split]
    pos = n_full + 2 * n_split
    w_refs = refs[pos:pos + n_full + n_split]
    pos += n_full + n_split
    x_refs = refs[pos:pos + 1 + x_split]
    pos += 1 + x_split
    mg_ref, g_ref = refs[pos:pos + 2]
    pos += 2
    if final:
        (y_ref,) = refs[pos:]
    else:
        mn_ref, xo_ref, h_ref = refs[pos:]
    i = pl.program_id(0)
    half = tm // 2

    def body(lhs_refs, x_ref):
        for r in range(2):
            rows = slice(r * half, (r + 1) * half)
            acc = jnp.dot(lhs_refs[0][rows, :], w_refs[0][...], preferred_element_type=F32)
            for a_ref, w_ref in zip(lhs_refs[1:], w_refs[1:]):
                acc = acc + jnp.dot(a_ref[rows, :], w_ref[...], preferred_element_type=F32)
            xn = x_ref[rows, :] + mg_ref[gate_row:gate_row + 1, :] * acc
            y = xn * lax.rsqrt(jnp.mean(xn * xn, -1, keepdims=True) + EPS) * g_ref[...]
            if final:
                y_ref[rows, :] = y
            else:
                xo_ref[rows, :] = xn
                h_ref[rows, :] = (y * (1.0 + mn_ref[1:2, :]) + mn_ref[0:1, :]).astype(h_ref.dtype)

    if n_split or x_split:
        n_ctx = NP_ROWS // tm

        @pl.when(i < n_ctx)
        def _():
            body(list(full_refs) + list(split_refs[0::2]), x_refs[0])

        @pl.when(i >= n_ctx)
        def _():
            body(list(full_refs) + list(split_refs[1::2]), x_refs[-1])
    else:
        body(list(full_refs), x_refs[0])


def _proj_res_norm(full, split, ws, x, mod_gate, gate_row, norm_g, mod_next, tm, vmem_mib, name,
                   row0=0, n_rows=N_TOK):
    final = mod_next is None
    n_ctx = NP_ROWS // tm
    off = row0 // tm
    x_split = isinstance(x, tuple)
    assert not ((split or x_split) and n_rows != N_TOK)
    ctx_map = lambda i: (jnp.minimum(i, n_ctx - 1), 0)
    lat_map = lambda i: (jnp.maximum(i - n_ctx, 0), 0)
    once = pl.Buffered(1)
    in_specs = [pl.BlockSpec((tm, a.shape[1]), lambda i: (i + off, 0)) for a in full]
    args = list(full)
    for a_ctx, a_lat in split:
        in_specs += [pl.BlockSpec((tm, a_ctx.shape[1]), ctx_map), pl.BlockSpec((tm, a_lat.shape[1]), lat_map)]
        args += [a_ctx, a_lat]
    in_specs += [pl.BlockSpec(w.shape, lambda i: (0, 0), pipeline_mode=once) for w in ws]
    args += list(ws)
    row_spec = pl.BlockSpec((tm, D_MODEL), lambda i: (i, 0))
    if x_split:
        in_specs += [pl.BlockSpec((tm, D_MODEL), ctx_map), pl.BlockSpec((tm, D_MODEL), lat_map)]
        args += list(x)
    else:
        in_specs.append(pl.BlockSpec((tm, D_MODEL), lambda i: (i + off, 0)))
        args.append(x)
    in_specs += [pl.BlockSpec((None, 6, D_MODEL), lambda i: (_modrow(i + off, tm), 0, 0)),
                 pl.BlockSpec((1, D_MODEL), lambda i: (0, 0))]
    args += [mod_gate, norm_g.reshape(1, D_MODEL)]
    if final:
        out_specs = row_spec
        out_shape = jax.ShapeDtypeStruct((n_rows, D_MODEL), F32)
    else:
        in_specs.append(pl.BlockSpec((None, 2, D_MODEL), lambda i: (_modrow(i + off, tm), 0, 0)))
        args.append(mod_next)
        out_specs = [row_spec, row_spec]
        out_shape = [jax.ShapeDtypeStruct((n_rows, D_MODEL), F32),
                     jax.ShapeDtypeStruct((n_rows, D_MODEL), BF16)]
    return pl.pallas_call(
        functools.partial(_proj_kernel, n_full=len(full), n_split=len(split), x_split=int(x_split),
                          gate_row=gate_row, final=final, tm=tm),
        grid=(n_rows // tm,),
        in_specs=in_specs,
        out_specs=out_specs,
        out_shape=out_shape,
        compiler_params=_cparams(1, vmem_mib),
        name=name,
    )(*args)


CONV_TM = 256
CONV_HALO = 16


def _convmod_kernel(a_ref, g_ref, ap_ref, gp_ref, an_ref, gn_ref, cw_ref, cb_ref, lng_ref, lnb_ref,
                    o_ref, ext_ref, y_ref, win_ref):
    tm = CONV_TM
    i = pl.program_id(0)
    n_ctx = NP_ROWS // tm
    per_seq = DEC_SEQ // tm
    j = (i - n_ctx) % per_seq
    latent = i >= n_ctx
    has_prev = jnp.logical_and(latent, j != 0)
    has_next = jnp.logical_and(latent, j != per_seq - 1)

    def glu(a, g):
        return a * jax.nn.sigmoid(g)

    ext_ref[CONV_HALO:CONV_HALO + tm, :] = glu(a_ref[...], g_ref[...])
    ext_ref[0:CONV_HALO, :] = jnp.where(has_prev, glu(ap_ref[...], gp_ref[...]), 0.0)
    ext_ref[CONV_HALO + tm:, :] = jnp.where(has_next, glu(an_ref[...], gn_ref[...]), 0.0)

    off = CONV_HALO - CONV_W // 2
    span = tm + SUBLANES * ((off + CONV_W - 1) // SUBLANES)
    hrows = tm // 2
    for cb in range(CONV_CH // LANES):
        cs = slice(cb * LANES, (cb + 1) * LANES)
        for r in range(SUBLANES):
            win_ref[r, 0:span, :] = ext_ref[r:r + span, cs]
        for base in range(0, tm, hrows):
            acc = jnp.zeros((hrows, LANES), F32) + cb_ref[:, cs]
            for k in range(CONV_W):
                r = (off + k) % SUBLANES
                m = (off + k) // SUBLANES * SUBLANES
                acc = acc + cw_ref[k:k + 1, cs] * win_ref[r, base + m:base + m + hrows, :]
            y_ref[base:base + hrows, cs] = acc

    y = y_ref[...]
    mu = jnp.mean(y, -1, keepdims=True)
    yc = y - mu
    z = yc * lax.rsqrt(jnp.mean(yc * yc, -1, keepdims=True) + EPS) * lng_ref[...] + lnb_ref[...]
    o_ref[...] = _silu(z).astype(o_ref.dtype)


def _convmod(p, cw, cb, ln_g, ln_b):
    tm = CONV_TM
    hb = tm // CONV_HALO
    last = N_TOK // CONV_HALO - 1
    prev_map = lambda i: (jnp.maximum(i * hb - 1, 0), 0)
    prev_map_g = lambda i: (jnp.maximum(i * hb - 1, 0), 1)
    next_map = lambda i: (jnp.minimum((i + 1) * hb, last), 0)
    next_map_g = lambda i: (jnp.minimum((i + 1) * hb, last), 1)
    vec = pl.BlockSpec((1, CONV_CH), lambda i: (0, 0))
    return pl.pallas_call(
        _convmod_kernel,
        grid=(N_TOK // tm,),
        in_specs=[pl.BlockSpec((tm, CONV_CH), lambda i: (i, 0)),
                  pl.BlockSpec((tm, CONV_CH), lambda i: (i, 1)),
                  pl.BlockSpec((CONV_HALO, CONV_CH), prev_map),
                  pl.BlockSpec((CONV_HALO, CONV_CH), prev_map_g),
                  pl.BlockSpec((CONV_HALO, CONV_CH), next_map),
                  pl.BlockSpec((CONV_HALO, CONV_CH), next_map_g),
                  pl.BlockSpec((CONV_W, CONV_CH), lambda i: (0, 0)),
                  vec, vec, vec],
        out_specs=pl.BlockSpec((tm, CONV_CH), lambda i: (i, 0)),
        out_shape=jax.ShapeDtypeStruct((N_TOK, CONV_CH), BF16),
        scratch_shapes=[pltpu.VMEM((tm + 2 * CONV_HALO, CONV_CH), F32),
                        pltpu.VMEM((tm, CONV_CH), F32),
                        pltpu.VMEM((SUBLANES, tm + 2 * CONV_HALO, LANES), F32)],
        compiler_params=_cparams(1, 32),
        name="convmod",
    )(p, p, p, p, p, p, cw, cb.reshape(1, -1), ln_g.reshape(1, -1), ln_b.reshape(1, -1))


ATT_GW = GQA_GROUP * HEAD_DIM
ATT_Q_BLK = 2 * CONV_CH // ATT_GW
ATT_K_BLK = (2 * CONV_CH + N_Q_HEADS * HEAD_DIM) // HEAD_DIM
ATT_V_BLK = ATT_K_BLK + N_KV_HEADS
ATT_TQ = 256


def _rms_rows(x, g):
    return x * lax.rsqrt(jnp.mean(x * x, -1, keepdims=True) + EPS) * g


def _rope(x, cos, sin):
    lane = lax.broadcasted_iota(jnp.int32, x.shape, 1)
    partner = jnp.where((lane & 1) == 0, pltpu.roll(x, HEAD_DIM - 1, 1), pltpu.roll(x, 1, 1))
    return x * cos + partner * sin


def _attend_group(q_ref, qg_ref, rotate, k, v, o_ref):
    for g in range(GQA_GROUP):
        hs = slice(g * HEAD_DIM, (g + 1) * HEAD_DIM)
        q = _rms_rows(q_ref[:, hs], qg_ref[...])
        if rotate is not None:
            q = rotate(q)
        q = (q * HEAD_DIM ** -0.5).astype(BF16)
        s = lax.dot_general(q, k, (((1,), (1,)), ((), ())), preferred_element_type=F32)
        p = jnp.exp(s - jnp.max(s, -1, keepdims=True))
        denom = jnp.sum(p, -1, keepdims=True)
        o = jnp.dot(p.astype(BF16), v, preferred_element_type=F32) / denom
        o_ref[:, hs] = o.astype(o_ref.dtype)


def _attn_ctx_kernel(q_ref, k_ref, v_ref, qg_ref, kg_ref, o_ref, kf_ref, vf_ref):
    kn = _rms_rows(k_ref[...], kg_ref[...])
    v = v_ref[...]
    kf_ref[...] = kn
    vf_ref[...] = v
    _attend_group(q_ref, qg_ref, None, kn.astype(BF16), v.astype(BF16), o_ref)


def _attn_ctx(p, qg, kg):
    nkv = N_KV_HEADS * HEAD_DIM
    vec = pl.BlockSpec((1, HEAD_DIM), lambda b, h: (0, 0))
    kv_out = pl.BlockSpec((SEQ, HEAD_DIM), lambda b, h: (b, h))
    return pl.pallas_call(
        _attn_ctx_kernel,
        grid=(BATCH, N_KV_HEADS),
        in_specs=[pl.BlockSpec((SEQ, ATT_GW), lambda b, h: (b, ATT_Q_BLK + h)),
                  pl.BlockSpec((SEQ, HEAD_DIM), lambda b, h: (b, ATT_K_BLK + h)),
                  pl.BlockSpec((SEQ, HEAD_DIM), lambda b, h: (b, ATT_V_BLK + h)),
                  vec, vec],
        out_specs=[pl.BlockSpec((SEQ, ATT_GW), lambda b, h: (b, h)), kv_out, kv_out],
        out_shape=[jax.ShapeDtypeStruct((NP_ROWS, N_Q_HEADS * HEAD_DIM), BF16),
                   jax.ShapeDtypeStruct((NP_ROWS, nkv), F32),
                   jax.ShapeDtypeStruct((NP_ROWS, nkv), F32)],
        compiler_params=_cparams(2, 32),
        name="attn_ctx",
    )(p, p, p, qg.reshape(1, -1), kg.reshape(1, -1))


def _attn_lat_kernel(q_ref, k_ref, v_ref, ck_ref, cv_ref, cosq_ref, sinq_ref, cosk_ref, sink_ref,
                     qg_ref, kg_ref, o_ref, ks_ref, vs_ref):
    @pl.when(pl.program_id(2) == 0)
    def _():
        for r0 in range(0, DEC_SEQ, ATT_TQ):
            rows = slice(r0, r0 + ATT_TQ)
            kn = _rms_rows(k_ref[rows, :], kg_ref[...])
            ks_ref[rows, :] = _rope(kn, cosk_ref[rows, :], sink_ref[rows, :]).astype(BF16)
            vs_ref[rows, :] = v_ref[rows, :].astype(BF16)
        ks_ref[DEC_SEQ:, :] = ck_ref[...].astype(BF16)
        vs_ref[DEC_SEQ:, :] = cv_ref[...].astype(BF16)

    cos = cosq_ref[...]
    sin = sinq_ref[...]
    _attend_group(q_ref, qg_ref, lambda x: _rope(x, cos, sin), ks_ref[...], vs_ref[...], o_ref)


def _attn_lat(p, cache_k, cache_v, cos_tab, sin_tab, qg, kg):
    tq = ATT_TQ
    n_q = DEC_SEQ // tq
    row0 = NP_ROWS // tq
    seq0 = NP_ROWS // DEC_SEQ
    vec = pl.BlockSpec((1, HEAD_DIM), lambda b, h, i: (0, 0))
    cache = pl.BlockSpec((None, PAST_LEN, HEAD_DIM), lambda b, h, i: (b, 0, h))
    tab_q = pl.BlockSpec((tq, HEAD_DIM), lambda b, h, i: (i, 0))
    tab_k = pl.BlockSpec((DEC_SEQ, HEAD_DIM), lambda b, h, i: (0, 0))
    return pl.pallas_call(
        _attn_lat_kernel,
        grid=(DEC_BATCH, N_KV_HEADS, n_q),
        in_specs=[pl.BlockSpec((tq, ATT_GW), lambda b, h, i: (row0 + b * n_q + i, ATT_Q_BLK + h)),
                  pl.BlockSpec((DEC_SEQ, HEAD_DIM), lambda b, h, i: (seq0 + b, ATT_K_BLK + h)),
                  pl.BlockSpec((DEC_SEQ, HEAD_DIM), lambda b, h, i: (seq0 + b, ATT_V_BLK + h)),
                  cache, cache, tab_q, tab_q, tab_k, tab_k, vec, vec],
        out_specs=pl.BlockSpec((tq, ATT_GW), lambda b, h, i: (b * n_q + i, h)),
        out_shape=jax.ShapeDtypeStruct((NS_ROWS, N_Q_HEADS * HEAD_DIM), BF16),
        scratch_shapes=[pltpu.VMEM((DEC_SEQ + PAST_LEN, HEAD_DIM), BF16),
                        pltpu.VMEM((DEC_SEQ + PAST_LEN, HEAD_DIM), BF16)],
        compiler_params=_cparams(3, 40),
        name="attn_lat",
    )(p, p, p, cache_k, cache_v, cos_tab, sin_tab, cos_tab, sin_tab, qg.reshape(1, -1), kg.reshape(1, -1))


def _rope_tables():
    rows = DEC_SEQ // GRID_W
    row = jnp.repeat(jnp.arange(rows), GRID_W).astype(F32)
    col = jnp.tile(jnp.arange(GRID_W), rows).astype(F32)
    n = HEAD_DIM // 4
    inv = ROPE_THETA ** (-jnp.arange(n, dtype=F32) / n)
    ang = jnp.concatenate([row[:, None] * inv, col[:, None] * inv], -1)
    cos = jnp.repeat(jnp.cos(ang), 2, axis=-1)
    sin = jnp.repeat(jnp.sin(ang), 2, axis=-1) * jnp.tile(jnp.array([-1.0, 1.0], F32), HEAD_DIM // 2)
    return cos, sin


GLA_CH = 128
GATE_TM = 512
GLA_TB = 256


def _gate_kernel(z_ref, wg_ref, bg_ref, of_ref, ob_ref):
    z = z_ref[...]
    r = lax.broadcasted_iota(jnp.int32, (GLA_CH, GLA_CH), 0)
    c = lax.broadcasted_iota(jnp.int32, (GLA_CH, GLA_CH), 1)
    for d, o_ref in enumerate((of_ref, ob_ref)):
        zz = jnp.dot(z, wg_ref[d], preferred_element_type=F32) + bg_ref[d]
        lg = (jnp.minimum(zz, 0.0) - jnp.log(1.0 + jnp.exp(-jnp.abs(zz)))) * (1.0 / GLA_TAU)
        hi = lg.astype(BF16)
        rest = lg - hi.astype(F32)
        mid = rest.astype(BF16)
        lo = (rest - mid.astype(F32)).astype(BF16)
        tri = jnp.where((r >= c) if d == 0 else (r <= c), 1.0, 0.0).astype(BF16)
        for r0 in range(0, GATE_TM, GLA_CH):
            rows = slice(r0, r0 + GLA_CH)
            o_ref[rows, :] = (jnp.dot(tri, hi[rows, :], preferred_element_type=F32)
                              + jnp.dot(tri, mid[rows, :], preferred_element_type=F32)
                              + jnp.dot(tri, lo[rows, :], preferred_element_type=F32))


def _gla_gates(z, wg_pad, bg):
    tm = GATE_TM
    return pl.pallas_call(
        _gate_kernel,
        grid=(N_TOK // tm,),
        in_specs=[pl.BlockSpec((tm, LANES), lambda i: (i, 0)),
                  pl.BlockSpec((2, LANES, GLA_HK), lambda i: (0, 0, 0)),
                  pl.BlockSpec((2, 1, GLA_HK), lambda i: (0, 0, 0))],
        out_specs=[pl.BlockSpec((tm, GLA_HK), lambda i: (i, 0))] * 2,
        out_shape=[jax.ShapeDtypeStruct((N_TOK, GLA_HK), F32)] * 2,
        compiler_params=_cparams(1, 32),
        name="gla_gate",
    )(z, wg_pad, bg.reshape(2, 1, GLA_HK))


def _gla_kernel(*refs, n_t, t_len, has_h0, want_hfin, n_alias, fill_layer):
    rest = list(refs)
    qf_ref, kf_ref, vf_ref, bf_ref = (rest.pop(0) for _ in range(4))
    if n_t == 1:
        qb_ref, kb_ref, vb_ref = qf_ref, kf_ref, vf_ref
    else:
        qb_ref, kb_ref, vb_ref = (rest.pop(0) for _ in range(3))
    bb_ref, r_ref, gn_ref = (rest.pop(0) for _ in range(3))
    h0f_ref = rest.pop(0) if has_h0 else None
    h0b_ref = rest.pop(0) if has_h0 else None
    rest = rest[n_alias:]
    y_ref = rest.pop(0)
    hff_ref = rest.pop(0) if want_hfin else None
    hfb_ref = rest.pop(0) if want_hfin else None
    htf_ref, htb_ref, o_ref = rest
    t = pl.program_id(2)
    L = GLA_CH
    mid = L // 2

    @pl.when(t == 0)
    def _():
        if has_h0:
            htf_ref[...] = h0f_ref[...].T
            htb_ref[...] = h0b_ref[...].T
        else:
            htf_ref[...] = jnp.zeros_like(htf_ref)
            htb_ref[...] = jnp.zeros_like(htb_ref)
        o_ref[...] = jnp.zeros_like(o_ref)

    row = lax.broadcasted_iota(jnp.int32, (L, L), 0)
    col = lax.broadcasted_iota(jnp.int32, (L, L), 1)
    nt = (((1,), (1,)), ((), ()))
    tn = (((0,), (0,)), ((), ()))

    def chunk(q_ref, k_ref, v_ref, b_ref, ht_ref, r0, out0, reverse):
        rows = slice(r0, r0 + L)
        q = q_ref[rows, :].astype(F32) * GLA_DK ** -0.5
        k = k_ref[rows, :].astype(F32)
        v = v_ref[rows, :]
        b = b_ref[rows, :]
        b_end = b[0:1, :] if reverse else b[L - 1:L, :]
        b_mid = b[mid:mid + 1, :] if reverse else b[mid - 1:mid, :]
        qe = (q * jnp.exp(b - b_mid)).astype(BF16)
        ke = (k * jnp.exp(b_mid - b)).astype(BF16)
        a = lax.dot_general(qe, ke, nt, preferred_element_type=F32)
        a = jnp.where((row <= col) if reverse else (row >= col), a, 0.0).astype(BF16)
        ht = ht_ref[...]
        o = lax.dot_general((q * jnp.exp(b)).astype(BF16), ht.astype(BF16), nt, preferred_element_type=F32)
        o = o + jnp.dot(a, v, preferred_element_type=F32)
        o_ref[pl.ds(out0, L), :] += o
        kd = (k * jnp.exp(b_end - b)).astype(BF16)
        ht_ref[...] = ht * jnp.exp(b_end) + lax.dot_general(v, kd, tn, preferred_element_type=F32)

    n_c = GLA_TB // L
    for c in range(n_c):
        chunk(qf_ref, kf_ref, vf_ref, bf_ref, htf_ref, c * L,
              pl.multiple_of(t * GLA_TB + c * L, L), False)
        cb = n_c - 1 - c
        chunk(qb_ref, kb_ref, vb_ref, bb_ref, htb_ref, cb * L,
              pl.multiple_of((n_t - 1 - t) * GLA_TB + cb * L, L), True)

    @pl.when(t == n_t - 1)
    def _():
        for r0 in range(0, t_len, GLA_TB):
            rows = slice(r0, r0 + GLA_TB)
            o = o_ref[rows, :]
            y = o * lax.rsqrt(jnp.mean(o * o, -1, keepdims=True) + EPS) * gn_ref[...]
            y_ref[rows, :] = (y * _silu(r_ref[rows, :].astype(F32))).astype(y_ref.dtype)
        if want_hfin and fill_layer is None:
            hff_ref[...] = htf_ref[...].T
            hfb_ref[...] = htb_ref[...].T
        elif want_hfin:
            for s in range(N_ODD):
                if s == fill_layer:
                    hff_ref[s] = htf_ref[...].T
                    hfb_ref[s] = htb_ref[...].T
                else:
                    hff_ref[s] = jnp.zeros((GLA_DK, GLA_DV), F32)
                    hfb_ref[s] = jnp.zeros((GLA_DK, GLA_DV), F32)


def _gla(p, bcum, norm_g, h0f, h0b, row0, n_b, t_len, want_hfin, name, layer=0, stacks=None):
    n_t = t_len // GLA_TB
    blk0 = row0 // GLA_TB
    seq0 = row0 // t_len
    has_h0 = h0f is not None
    k_blk = GLA_HK // GLA_DK
    v_blk = 2 * GLA_HK // GLA_DV
    r_blk = (2 * GLA_HK + GLA_HV) // GLA_DV

    def fwd(b_, t):
        return blk0 + b_ * n_t + t

    def bwd(b_, t):
        return blk0 + b_ * n_t + (n_t - 1 - t)

    def qkv_specs(rb):
        return [pl.BlockSpec((GLA_TB, GLA_DK), lambda b_, h, t: (rb(b_, t), h)),
                pl.BlockSpec((GLA_TB, GLA_DK), lambda b_, h, t: (rb(b_, t), k_blk + h)),
                pl.BlockSpec((GLA_TB, GLA_DV), lambda b_, h, t: (rb(b_, t), v_blk + h))]

    def decay_spec(rb):
        return pl.BlockSpec((GLA_TB, GLA_DK), lambda b_, h, t: (rb(b_, t), h))

    state = pl.BlockSpec((None, None, GLA_DK, GLA_DV), lambda b_, h, t: (b_, h, 0, 0))
    in_specs = qkv_specs(fwd) + [decay_spec(fwd)]
    args = [p, p, p, bcum[0]]
    if n_t > 1:
        in_specs += qkv_specs(bwd)
        args += [p, p, p]
    in_specs += [decay_spec(bwd),
                 pl.BlockSpec((t_len, GLA_DV), lambda b_, h, t: (seq0 + b_, r_blk + h)),
                 pl.BlockSpec((1, GLA_DV), lambda b_, h, t: (0, 0))]
    args += [bcum[1], p, norm_g.reshape(1, GLA_DV)]
    if has_h0:
        in_specs += [state, state]
        args += [h0f, h0b]
    out_specs = [pl.BlockSpec((t_len, GLA_DV), lambda b_, h, t: (b_, h))]
    out_shape = [jax.ShapeDtypeStruct((n_b * t_len, GLA_HV), BF16)]
    aliases = {}
    fill_layer = None
    if want_hfin:
        if stacks is None:
            fill_layer = layer
            slot = pl.BlockSpec((None, N_ODD, None, GLA_DK, GLA_DV), lambda b_, h, t: (b_, 0, h, 0, 0))
        else:
            slot = pl.BlockSpec((None, None, None, GLA_DK, GLA_DV), lambda b_, h, t: (b_, layer, h, 0, 0))
            aliases = {len(args): 1, len(args) + 1: 2}
            in_specs += [pl.BlockSpec(memory_space=pl.ANY)] * 2
            args += list(stacks)
        out_specs += [slot, slot]
        out_shape += [jax.ShapeDtypeStruct((n_b, N_ODD, GLA_HEADS, GLA_DK, GLA_DV), F32)] * 2
    return pl.pallas_call(
        functools.partial(_gla_kernel, n_t=n_t, t_len=t_len, has_h0=has_h0, want_hfin=want_hfin,
                          n_alias=len(aliases), fill_layer=fill_layer),
        input_output_aliases=aliases,
        grid=(n_b, GLA_HEADS, n_t),
        in_specs=in_specs,
        out_specs=out_specs,
        out_shape=out_shape,
        scratch_shapes=[pltpu.VMEM((GLA_DV, GLA_DK), F32), pltpu.VMEM((GLA_DV, GLA_DK), F32),
                        pltpu.VMEM((t_len, GLA_DV), F32)],
        compiler_params=_cparams(3, 40),
        name=name,
    )(*args)


FFN_TM = 2048
FFN_TN = 256


def _ffn_up_kernel(h_ref, wg_ref, wv_ref, cwg_ref, cwv_ref, cbg_ref, cbv_ref, o_ref):
    tm = FFN_TM
    i = pl.program_id(0)
    seq = jnp.where(i < NP_ROWS // tm, SEQ, DEC_SEQ)
    sub = lax.broadcasted_iota(jnp.int32, (SUBLANES, FFN_TN), 0)
    h = h_ref[...]

    def zero_at_edges(x, first):
        pieces = []
        for t in range(0, tm, SEQ):
            seg = x[t:t + SEQ]
            if first:
                at_edge = (t & (seq - 1)) == 0
                pieces += [jnp.where(at_edge & (sub == 0), 0, seg[:SUBLANES]), seg[SUBLANES:]]
            else:
                at_edge = ((t + SEQ) & (seq - 1)) == 0
                pieces += [seg[:SEQ - SUBLANES],
                           jnp.where(at_edge & (sub == SUBLANES - 1), 0, seg[SEQ - SUBLANES:])]
        return jnp.concatenate(pieces, 0)

    def pair(g, v):
        return pltpu.pack_elementwise([g, v], packed_dtype=BF16)

    def as_bf16(words):
        return pltpu.bitcast(words, BF16)

    def per_channel(g_row, v_row):
        words = pair(jnp.broadcast_to(g_row, (SUBLANES, FFN_TN)), jnp.broadcast_to(v_row, (SUBLANES, FFN_TN)))
        return jnp.tile(as_bf16(words), (tm // SUBLANES, 1))

    u = pair(jnp.dot(h, wg_ref[...].astype(BF16), preferred_element_type=F32),
             jnp.dot(h, wv_ref[...].astype(BF16), preferred_element_type=F32))
    up = zero_at_edges(pltpu.roll(u, 1, 0), True)
    un = zero_at_edges(pltpu.roll(u, tm - 1, 0), False)
    y = (per_channel(cwg_ref[0:1, :], cwv_ref[0:1, :]) * as_bf16(up)
         + per_channel(cwg_ref[1:2, :], cwv_ref[1:2, :]) * as_bf16(u)
         + per_channel(cwg_ref[2:3, :], cwv_ref[2:3, :]) * as_bf16(un)
         + per_channel(cbg_ref[...], cbv_ref[...]))
    y = pltpu.bitcast(y, jnp.int32)
    g = pltpu.unpack_elementwise(y, index=0, packed_dtype=BF16, unpacked_dtype=F32)
    val = pltpu.unpack_elementwise(y, index=1, packed_dtype=BF16, unpacked_dtype=F32)
    o_ref[...] = (_silu(g) * val).astype(o_ref.dtype)


def _ffn_up(h, w_up_all, layer, cw, cb):
    tm, tn = FFN_TM, FFN_TN
    nj = D_FF // tn
    return pl.pallas_call(
        _ffn_up_kernel,
        grid=(N_TOK // tm, nj),
        in_specs=[pl.BlockSpec((tm, D_MODEL), lambda i, j: (i, 0)),
                  pl.BlockSpec((None, D_MODEL, tn), lambda i, j: (layer, 0, j)),
                  pl.BlockSpec((None, D_MODEL, tn), lambda i, j: (layer, 0, nj + j)),
                  pl.BlockSpec((3, tn), lambda i, j: (0, j)),
                  pl.BlockSpec((3, tn), lambda i, j: (0, nj + j)),
                  pl.BlockSpec((1, tn), lambda i, j: (0, j)),
                  pl.BlockSpec((1, tn), lambda i, j: (0, nj + j))],
        out_specs=pl.BlockSpec((tm, tn), lambda i, j: (i, j)),
        out_shape=jax.ShapeDtypeStruct((N_TOK, D_FF), BF16),
        compiler_params=_cparams(2, 48),
        name="ffn_up",
    )(h, w_up_all, w_up_all, cw, cw, cb.reshape(1, -1), cb.reshape(1, -1))


def kernel(x_prompt, x_sample, c, cache_k, cache_v, state_gla_fwd, state_gla_bwd, c_ctx, norm1_g, norm2_g,
           w_ada, b_ada, even_w_in, conv_w, conv_b, conv_ln_g, conv_ln_b, q_norm_g, k_norm_g, even_w_out,
           odd_w_in, gla_w_gate, gla_b_gate, gla_norm_g, odd_w_out, ffn_w_up, ffn_conv_w, ffn_conv_b,
           ffn_w_down, final_norm_g):
    x = (x_prompt.reshape(NP_ROWS, D_MODEL), x_sample.reshape(NS_ROWS, D_MODEL))
    cond = jnp.concatenate([c_ctx[None, :], c, jnp.zeros((N_COND - 1 - DEC_BATCH, D_MODEL), F32)], 0)
    mod_all = _ada(cond, w_ada, b_ada).reshape(DEPTH, N_COND, 6, D_MODEL)
    cos_tab, sin_tab = _rope_tables()
    nkv = N_KV_HEADS * HEAD_DIM

    new_k, new_v, state_stacks = [], [], None
    h = _modnorm(x[0], x[1], norm1_g[0], mod_all[0], 0, 1)
    for l in range(DEPTH):
        mod = mod_all[l]
        mod_norm2 = mod[:, 3:5]
        if l % 2 == 0:
            e = l // 2
            p = _matmul(h, even_w_in[e].astype(BF16), F32, 1024, EVEN_IN // 2, 48, "even_in")
            a = _convmod(p, conv_w[e], conv_b[e], conv_ln_g[e], conv_ln_b[e])
            o_p, k_new, v_new = _attn_ctx(p, q_norm_g[e], k_norm_g[e])
            new_k.append(k_new.reshape(BATCH, SEQ, N_KV_HEADS, HEAD_DIM))
            new_v.append(v_new.reshape(BATCH, SEQ, N_KV_HEADS, HEAD_DIM))
            o_s = _attn_lat(p, cache_k[:, e].reshape(DEC_BATCH, PAST_LEN, nkv),
                            cache_v[:, e].reshape(DEC_BATCH, PAST_LEN, nkv), cos_tab, sin_tab,
                            q_norm_g[e], k_norm_g[e])
            w_out = even_w_out[e].astype(BF16)
            x, h2 = _proj_res_norm([a], [(o_p, o_s)], [w_out[:CONV_CH], w_out[CONV_CH:]], x, mod, 2,
                                   norm2_g[l], mod_norm2, 512, 48, "even_out")
        else:
            od = l // 2
            w_in = odd_w_in[od]
            p = _matmul(h, w_in[:, :GLA_MAIN].astype(BF16), BF16, 1024, 1024, 40, "odd_in")
            w_z = jnp.pad(w_in[:, GLA_MAIN:], ((0, 0), (0, LANES - 2 * GLA_RANK))).astype(BF16)
            z = _matmul(h, w_z, BF16, 1024, LANES, 32, "odd_z")
            wg = gla_w_gate[od]
            wg_pad = jnp.zeros((2, LANES, GLA_HK), F32)
            wg_pad = wg_pad.at[0, :GLA_RANK].set(wg[0]).at[1, GLA_RANK:2 * GLA_RANK].set(wg[1]).astype(BF16)
            bcum = _gla_gates(z, wg_pad, gla_b_gate[od])
            y_p, *state_stacks = _gla(p, bcum, gla_norm_g[od], None, None, 0, BATCH, SEQ, True, "gla_ctx", od,
                                      state_stacks)
            (y_s,) = _gla(p, bcum, gla_norm_g[od], state_gla_fwd[:, od], state_gla_bwd[:, od], NP_ROWS,
                          DEC_BATCH, DEC_SEQ, False, "gla_lat")
            x, h2 = _proj_res_norm([], [(y_p, y_s)], [odd_w_out[od].astype(BF16)], x, mod, 2, norm2_g[l],
                                   mod_norm2, 512, 48, "odd_out")
        u = _ffn_up(h2, ffn_w_up, l, ffn_conv_w[l], ffn_conv_b[l])
        w_down = ffn_w_down[l].astype(BF16)
        if l + 1 < DEPTH:
            x, h = _proj_res_norm([u], [], [w_down], x, mod, 5, norm1_g[l + 1], mod_all[l + 1][:, 0:2],
                                  256, 56, "ffn_down")
        else:
            y_p = _proj_res_norm([u], [], [w_down], x, mod, 5, final_norm_g, None, 256, 56, "ffn_down_ctx",
                                 0, NP_ROWS)
            y_s = _proj_res_norm([u], [], [w_down], x, mod, 5, final_norm_g, None, 256, 56, "ffn_down_lat",
                                 NP_ROWS, NS_ROWS)

    y_prompt = y_p.reshape(BATCH, SEQ, D_MODEL)
    y_sample = y_s.reshape(DEC_BATCH, DEC_SEQ, D_MODEL)
    return (y_prompt, y_sample, jnp.stack(new_k, 1), jnp.stack(new_v, 1), state_stacks[0], state_stacks[1])
```
